```python
import math
import jax, jax.numpy as jnp
from jax import lax
import numpy as np

D_MODEL = 1024
BATCH = 32
SEQ = 2048
DEPTH = 2
DEC_BATCH = 16
DEC_SEQ = 2048
PAST_LEN = 128

MLA_HEADS = 8
MLA_Q_LORA = 384
MLA_KV_LORA = 256
MLA_NOPE = 64
MLA_ROPE = 32
MLA_V = 64
MLA_QK = MLA_NOPE + MLA_ROPE
ROPE_THETA = 10000.0
Q_BLOCK = 128
RG_WIDTH = 512
RG_BLOCKS = 8
RG_BW = RG_WIDTH // RG_BLOCKS
RG_CONV = 4
RG_C = 8.0
GDN_HEADS = 4
GDN_DK = 128
GDN_DV = 128
GDN_CONV = 4
GDN_CHUNK = 64
N_BRANCH = 3
BRANCH_WIDTH = 512
D_FF = 2816
FFN_CONV = 3
EPS = 1e-6

IN_SPLITS = (MLA_Q_LORA, MLA_KV_LORA + MLA_ROPE, RG_WIDTH, RG_WIDTH,
             GDN_HEADS * (2 * GDN_DK + GDN_DV), GDN_HEADS * GDN_DV,
             2 * GDN_HEADS, 2 * GDN_HEADS, N_BRANCH * D_MODEL)
N_IN = sum(IN_SPLITS)

kernel_name = 'hybrid_mla_rglru_gdn_encoder'


def rmsnorm(x, g):
    xf = x.astype(jnp.float32)
    y = xf * lax.rsqrt(jnp.mean(xf * xf, axis=-1, keepdims=True) + EPS)
    return (y * g.astype(jnp.float32)).astype(x.dtype)


def l2norm(t):
    return t * lax.rsqrt(jnp.sum(t * t, axis=-1, keepdims=True) + EPS)


def split_cols(t, sizes):
    out, start = [], 0
    for n in sizes:
        out.append(t[..., start:start + n])
        start += n
    return out


def dwconv_centred(x, w, b=None):
    K = w.shape[0]
    lo = (K - 1) // 2
    hi = K - 1 - lo
    y = lax.conv_general_dilated(x, w[:, None, :].astype(x.dtype), window_strides=(1,),
                                 padding=[(lo, hi)], dimension_numbers=('NWC', 'WIO', 'NWC'),
                                 feature_group_count=x.shape[-1])
    if b is not None:
        y = y + b
    return y


def rope_tables(S):
    inv = 1.0 / (ROPE_THETA ** (jnp.arange(0, MLA_ROPE, 2, dtype=jnp.float32) / MLA_ROPE))
    ang = jnp.arange(S, dtype=jnp.float32)[:, None] * inv[None, :]
    return jnp.cos(ang), jnp.sin(ang)


def apply_rope(x, cos, sin):
    half = x.shape[-1] // 2
    x1 = x[..., :half].astype(jnp.float32)
    x2 = x[..., half:].astype(jnp.float32)
    return jnp.concatenate([x1 * cos - x2 * sin, x2 * cos + x1 * sin], axis=-1).astype(x.dtype)


def mla_branch(q_down, kv_down, cos, sin, q_norm_g, w_uq, kv_norm_g, w_ukv):
    B, S, _ = q_down.shape
    cq = rmsnorm(q_down, q_norm_g)
    q = (cq @ w_uq).reshape(B, S, MLA_HEADS, MLA_QK)
    q_rope = apply_rope(q[..., MLA_NOPE:], cos[:, None, :], sin[:, None, :])
    q = jnp.concatenate([q[..., :MLA_NOPE], q_rope], axis=-1) * (MLA_QK ** -0.5)
    ckv = rmsnorm(kv_down[..., :MLA_KV_LORA], kv_norm_g)
    k_rope = apply_rope(kv_down[..., MLA_KV_LORA:], cos, sin)
    kv = (ckv @ w_ukv).reshape(B, S, MLA_HEADS, MLA_NOPE + MLA_V)
    v = kv[..., MLA_NOPE:]
    k = jnp.concatenate([kv[..., :MLA_NOPE],
                         jnp.broadcast_to(k_rope[:, :, None, :], (B, S, MLA_HEADS, MLA_ROPE))], axis=-1)
    nq = S // Q_BLOCK
    qb = jnp.moveaxis(q.reshape(B, nq, Q_BLOCK, MLA_HEADS, MLA_QK), 1, 0)

    def attend(q_blk):
        s = jnp.einsum('bqhd,bkhd->bhqk', q_blk, k).astype(jnp.float32)
        p = jax.nn.softmax(s, axis=-1).astype(v.dtype)
        return jnp.einsum('bhqk,bkhd->bqhd', p, v)

    o = lax.map(attend, qb)
    return jnp.moveaxis(o, 0, 1).reshape(B, S, MLA_HEADS * MLA_V)


def lin_combine(e1, e2):
    a1, b1 = e1
    a2, b2 = e2
    return a1 * a2, a2 * b1 + b2


def rglru_branch(x_in, gate_in, conv_w, conv_b, w_a, b_a, w_i, b_i, lam):
    B, S, _ = x_in.shape
    xc = dwconv_centred(x_in, conv_w, conv_b)
    xb = xc.reshape(B, S, RG_BLOCKS, RG_BW)
    xf = xc.astype(jnp.float32)
    pos = jnp.arange(S)

    def direction(d, reverse):
        r = jax.nn.sigmoid((jnp.einsum('bsnc,ncd->bsnd', xb, w_a[d]).reshape(B, S, RG_WIDTH) + b_a[d]).astype(jnp.float32))
        i = jax.nn.sigmoid((jnp.einsum('bsnc,ncd->bsnd', xb, w_i[d]).reshape(B, S, RG_WIDTH) + b_i[d]).astype(jnp.float32))
        log_a = -RG_C * r * jax.nn.softplus(-lam[d].astype(jnp.float32))
        a = jnp.exp(log_a)
        first = (pos == ((S - 1) if reverse else 0))[None, :, None]
        mult = jnp.where(first, 1.0, jnp.sqrt(jnp.maximum(-jnp.expm1(2.0 * log_a), 0.0)))
        _, h = lax.associative_scan(lin_combine, (a, mult * (i * xf)), reverse=reverse, axis=1)
        return h

    h = direction(0, False) + direction(1, True)
    return (h * jax.nn.gelu(gate_in.astype(jnp.float32))).astype(x_in.dtype)


def gated_delta_chunked(q, k, v, g, beta):
    B, S, H, DK = q.shape
    DV = v.shape[-1]
    C = GDN_CHUNK
    N = S // C

    def chunks(t):
        t = t.reshape((B, N, C, H) + t.shape[3:])
        return jnp.moveaxis(t, (1, 3), (0, 2))

    q = chunks(q) * (DK ** -0.5)
    k = chunks(k)
    v = chunks(v)
    beta = chunks(beta)
    g = jnp.cumsum(chunks(g), axis=-1)
    tril = jnp.tril(jnp.ones((C, C), dtype=bool))
    strict = jnp.tril(jnp.ones((C, C), dtype=bool), -1)
    decay = jnp.exp(jnp.where(tril, g[..., :, None] - g[..., None, :], -jnp.inf))
    kb = k * beta[..., None]
    A = jnp.where(strict, jnp.einsum('nbhid,nbhjd->nbhij', kb, k) * decay, 0.0)
    eye = jnp.eye(C, dtype=jnp.float32)
    T = lax.linalg.triangular_solve(eye + A, jnp.broadcast_to(eye, A.shape), left_side=True,
                                    lower=True, unit_diagonal=True)
    u = T @ (v * beta[..., None])
    w = T @ (kb * jnp.exp(g)[..., None])
    attn = jnp.einsum('nbhid,nbhjd->nbhij', q, k) * decay
    q_dec = q * jnp.exp(g)[..., None]
    k_dec = k * jnp.exp(g[..., -1:] - g)[..., None]
    g_last = jnp.exp(g[..., -1])

    def step(state, xs):
        u_n, w_n, qd_n, kd_n, attn_n, gl_n = xs
        v_new = u_n - w_n @ state
        o_n = qd_n @ state + attn_n @ v_new
        state = state * gl_n[..., None, None] + jnp.swapaxes(kd_n, -1, -2) @ v_new
        return state, o_n

    s0 = jnp.zeros((B, H, DK, DV), jnp.float32)
    _, o = lax.scan(step, s0, (u, w, q_dec, k_dec, attn, g_last))
    return jnp.moveaxis(o, (0, 2), (1, 3)).reshape(B, S, H, DV)


def gdn_branch(qkv, z, alpha_raw, beta_raw, conv_w, a_log, dt_bias, norm_g):
    B, S, _ = qkv.shape
    qkv_c = jax.nn.silu(dwconv_centred(qkv, conv_w)).astype(jnp.float32)
    q, k, v = split_cols(qkv_c, (GDN_HEADS * GDN_DK, GDN_HEADS * GDN_DK, GDN_HEADS * GDN_DV))
    q = l2norm(q.reshape(B, S, GDN_HEADS, GDN_DK))
    k = l2norm(k.reshape(B, S, GDN_HEADS, GDN_DK))
    v = v.reshape(B, S, GDN_HEADS, GDN_DV)
    alpha_raw = alpha_raw.astype(jnp.float32).reshape(B, S, 2, GDN_HEADS)
    beta = jax.nn.sigmoid(beta_raw.astype(jnp.float32).reshape(B, S, 2, GDN_HEADS))
    g = -jnp.exp(a_log.astype(jnp.float32)) * jax.nn.softplus(alpha_raw + dt_bias.astype(jnp.float32))
    o_f = gated_delta_chunked(q, k, v, g[:, :, 0], beta[:, :, 0])
    fl = lambda t: jnp.flip(t, axis=1)
    o_b = fl(gated_delta_chunked(fl(q), fl(k), fl(v), fl(g[:, :, 1]), fl(beta[:, :, 1])))
    o = rmsnorm(o_f + o_b, norm_g) * jax.nn.silu(z.astype(jnp.float32).reshape(B, S, GDN_HEADS, GDN_DV))
    return o.reshape(B, S, GDN_HEADS * GDN_DV).astype(qkv.dtype)


def encoder_layer(x, c, cos, sin, p):
    B, S, D = x.shape
    mod = (jax.nn.silu(c) @ p['w_mod'] + p['b_mod'])[:, None, :]
    sh1, sc1, gt1, sh2, sc2, gt2 = jnp.split(mod, 6, axis=-1)
    h = rmsnorm(x, p['ln1_g']) * (1 + sc1) + sh1
    (q_down, kv_down, rg_x, rg_gate, gdn_qkv, gdn_z,
     gdn_alpha, gdn_beta, gates) = split_cols(h @ p['w_in'], IN_SPLITS)
    o_mla = mla_branch(q_down, kv_down, cos, sin, p['mla_q_norm_g'], p['mla_w_uq'],
                       p['mla_kv_norm_g'], p['mla_w_ukv'])
    o_rg = rglru_branch(rg_x, rg_gate, p['rg_conv_w'], p['rg_conv_b'], p['rg_w_a'], p['rg_b_a'],
                        p['rg_w_i'], p['rg_b_i'], p['rg_lam'])
    o_gdn = gdn_branch(gdn_qkv, gdn_z, gdn_alpha, gdn_beta, p['gdn_conv_w'], p['gdn_a_log'],
                       p['gdn_dt_bias'], p['gdn_norm_g'])
    gates = jax.nn.sigmoid(gates.reshape(B, S, N_BRANCH, D).astype(jnp.float32)).astype(x.dtype)
    mixed = None
    for n, o_n in enumerate((o_mla, o_rg, o_gdn)):
        term = gates[:, :, n] * (o_n @ p['w_branch'][n])
        mixed = term if mixed is None else mixed + term
    x = x + gt1 * (mixed @ p['w_out'])
    h2 = rmsnorm(x, p['ln2_g']) * (1 + sc2) + sh2
    up = dwconv_centred(h2 @ p['ffn_w_up'], p['ffn_conv_w'], p['ffn_conv_b'])
    u_lin, u_gate = jnp.split(up, 2, axis=-1)
    x = x + gt2 * ((jax.nn.silu(u_gate) * u_lin) @ p['ffn_w_down'])
    return x


def trunk(x, c, stacked, final_norm_g):
    cos, sin = rope_tables(x.shape[1])
    for l in range(DEPTH):
        p = {name: arr[l] for name, arr in stacked.items()}
        x = encoder_layer(x, c, cos, sin, p)
    return rmsnorm(x, final_norm_g)


def setup_inputs(seed: int = 0) -> dict:
    key = jax.random.key(seed)
    ks = iter(jax.random.split(key, 48))
    f32 = jnp.float32
    L, D = DEPTH, D_MODEL

    def nrm(shape, scale):
        return jax.random.normal(next(ks), shape, f32) * scale

    def gain(shape):
        return 1.0 + 0.02 * jax.random.normal(next(ks), shape, f32)

    u = jax.random.uniform(next(ks), (L, 2, RG_WIDTH), f32, 0.9, 0.999)
    rg_lam = jnp.log(u) - jnp.log1p(-u)
    gdn_a_log = jnp.log(jax.random.uniform(next(ks), (L, 2, GDN_HEADS), f32, 1.0, 16.0))
    dt = jnp.exp(jax.random.uniform(next(ks), (L, 2, GDN_HEADS), f32, math.log(1e-3), math.log(1e-1)))
    gdn_dt_bias = dt + jnp.log(-jnp.expm1(-dt))
    return {
        'x_prompt': nrm((BATCH, SEQ, D), 1.0),
        'x_sample': nrm((DEC_BATCH, DEC_SEQ, D), 1.0),
        'c_prompt': nrm((BATCH, D), 1.0),
        'c_sample': nrm((DEC_BATCH, D), 1.0),
        'ln1_g': gain((L, D)),
        'w_mod': nrm((L, D, 6 * D), D ** -0.5),
        'b_mod': nrm((L, 6 * D), 0.02),
        'w_in': nrm((L, D, N_IN), D ** -0.5),
        'mla_q_norm_g': gain((L, MLA_Q_LORA)),
        'mla_w_uq': nrm((L, MLA_Q_LORA, MLA_HEADS * MLA_QK), MLA_Q_LORA ** -0.5),
        'mla_kv_norm_g': gain((L, MLA_KV_LORA)),
        'mla_w_ukv': nrm((L, MLA_KV_LORA, MLA_HEADS * (MLA_NOPE + MLA_V)), MLA_KV_LORA ** -0.5),
        'rg_conv_w': nrm((L, RG_CONV, RG_WIDTH), RG_CONV ** -0.5),
        'rg_conv_b': nrm((L, RG_WIDTH), 0.02),
        'rg_w_a': nrm((L, 2, RG_BLOCKS, RG_BW, RG_BW), RG_BW ** -0.5),
        'rg_b_a': nrm((L, 2, RG_WIDTH), 0.02),
        'rg_w_i': nrm((L, 2, RG_BLOCKS, RG_BW, RG_BW), RG_BW ** -0.5),
        'rg_b_i': nrm((L, 2, RG_WIDTH), 0.02),
        'rg_lam': rg_lam,
        'gdn_conv_w': nrm((L, GDN_CONV, GDN_HEADS * (2 * GDN_DK + GDN_DV)), GDN_CONV ** -0.5),
        'gdn_a_log': gdn_a_log,
        'gdn_dt_bias': gdn_dt_bias,
        'gdn_norm_g': gain((L, GDN_DV)),
        'w_branch': nrm((L, N_BRANCH, BRANCH_WIDTH, D), BRANCH_WIDTH ** -0.5),
        'w_out': nrm((L, D, D), D ** -0.5),
        'ln2_g': gain((L, D)),
        'ffn_w_up': nrm((L, D, 2 * D_FF), D ** -0.5),
        'ffn_conv_w': nrm((L, FFN_CONV, 2 * D_FF), FFN_CONV ** -0.5),
        'ffn_conv_b': nrm((L, 2 * D_FF), 0.02),
        'ffn_w_down': nrm((L, D_FF, D), D_FF ** -0.5),
        'final_norm_g': gain((D,)),
    }


def reference(x_prompt, x_sample, c_prompt, c_sample, ln1_g, w_mod, b_mod, w_in,
              mla_q_norm_g, mla_w_uq, mla_kv_norm_g, mla_w_ukv,
              rg_conv_w, rg_conv_b, rg_w_a, rg_b_a, rg_w_i, rg_b_i, rg_lam,
              gdn_conv_w, gdn_a_log, gdn_dt_bias, gdn_norm_g,
              w_branch, w_out, ln2_g, ffn_w_up, ffn_conv_w, ffn_conv_b, ffn_w_down,
              final_norm_g):
    stacked = dict(ln1_g=ln1_g, w_mod=w_mod, b_mod=b_mod, w_in=w_in,
                   mla_q_norm_g=mla_q_norm_g, mla_w_uq=mla_w_uq,
                   mla_kv_norm_g=mla_kv_norm_g, mla_w_ukv=mla_w_ukv,
                   rg_conv_w=rg_conv_w, rg_conv_b=rg_conv_b, rg_w_a=rg_w_a, rg_b_a=rg_b_a,
                   rg_w_i=rg_w_i, rg_b_i=rg_b_i, rg_lam=rg_lam,
                   gdn_conv_w=gdn_conv_w, gdn_a_log=gdn_a_log, gdn_dt_bias=gdn_dt_bias,
                   gdn_norm_g=gdn_norm_g, w_branch=w_branch, w_out=w_out, ln2_g=ln2_g,
                   ffn_w_up=ffn_w_up, ffn_conv_w=ffn_conv_w, ffn_conv_b=ffn_conv_b,
                   ffn_w_down=ffn_w_down)
    y_prompt = trunk(x_prompt, c_prompt, stacked, final_norm_g)
    y_sample = trunk(x_sample, c_sample, stacked, final_norm_g)
    return (y_prompt, y_sample)
```

```python
import functools
import math

import jax
import jax.numpy as jnp
import numpy as np
from jax import lax
from jax.experimental import pallas as pl
from jax.experimental.pallas import tpu as pltpu

D_MODEL = 1024
DEPTH = 2
MLA_HEADS = 8
MLA_Q_LORA = 384
MLA_KV_LORA = 256
MLA_NOPE = 64
MLA_ROPE = 32
MLA_V = 64
MLA_QK = MLA_NOPE + MLA_ROPE
ROPE_THETA = 10000.0
RG_WIDTH = 512
RG_BLOCKS = 8
RG_C = 8.0
GDN_HEADS = 4
GDN_DK = 128
GDN_DV = 128
GDN_CHUNK = 64
D_FF = 2816
EPS = 1e-6

LANE = 128
HEAD_PAD = 128
VMEM_LIMIT = 56 * 1024 * 1024
FFN_CK = 256
N_STREAMS = 2 * GDN_HEADS

F32 = jnp.float32
BF16 = jnp.bfloat16


def _cparams(*sem):
    return pltpu.CompilerParams(dimension_semantics=sem, vmem_limit_bytes=VMEM_LIMIT)


def _const_spec(shape):
    nd = len(shape)
    return pl.BlockSpec(shape, lambda *_: (0,) * nd, pipeline_mode=pl.Buffered(1))


def _sigmoid(x):
    return 1.0 / (1.0 + jnp.exp(-x))


def _silu(x):
    return x * _sigmoid(x)


def _softplus(x):
    return jnp.maximum(x, 0.0) + jnp.log(1.0 + jnp.exp(-jnp.abs(x)))


def _gelu_tanh(x):
    c = math.sqrt(2.0 / math.pi)
    return 0.5 * x * (1.0 + jnp.tanh(c * (x + 0.044715 * (x * x * x))))


def _rms(x, g):
    return x * lax.rsqrt(jnp.mean(x * x, axis=-1, keepdims=True) + EPS) * g


def _dot(a, b):
    return jnp.dot(a, b, preferred_element_type=F32)


def _dot_nt(a, b):
    return lax.dot_general(a, b, (((1,), (1,)), ((), ())), preferred_element_type=F32)


def _dot_tn(a, b):
    return lax.dot_general(a, b, (((0,), (0,)), ((), ())), preferred_element_type=F32)


def _split3(x):
    x1 = x.astype(BF16)
    r1 = x - x1.astype(F32)
    x2 = r1.astype(BF16)
    x3 = (r1 - x2.astype(F32)).astype(BF16)
    return x1, x2, x3


def _mod_kernel(c_ref, w_ref, b_ref, o_ref):
    c = c_ref[...]
    a1, a2, a3 = _split3(_silu(c))
    w1, w2, w3 = _split3(w_ref[...])
    acc = _dot(a1, w1) + (_dot(a1, w2) + _dot(a2, w1)) + (_dot(a1, w3) + _dot(a2, w2) + _dot(a3, w1))
    o_ref[...] = acc + b_ref[...]


def _modulation(c, w_mod, b_mod):
    L, D, N = w_mod.shape
    Bt = c.shape[0]
    tn = 1024
    return pl.pallas_call(
        _mod_kernel,
        grid=(L, N // tn),
        in_specs=[pl.BlockSpec((Bt, D), lambda l, j: (0, 0)),
                  pl.BlockSpec((None, D, tn), lambda l, j: (l, 0, j)),
                  pl.BlockSpec((None, 1, tn), lambda l, j: (l, 0, j))],
        out_specs=pl.BlockSpec((None, Bt, tn), lambda l, j: (l, 0, j)),
        out_shape=jax.ShapeDtypeStruct((L, Bt, N), F32),
        compiler_params=_cparams("arbitrary", "arbitrary"),
        name="modulation",
    )(c, w_mod, b_mod.reshape(L, 1, N))


W_MLA, W_RG, W_GDN, W_AB, W_GATES = 768, 1024, 2048, 128, 3 * D_MODEL
_IN_OFFS = np.cumsum([0, W_MLA, W_RG, W_GDN, W_AB, W_GATES])


def _inproj_kernel(x_ref, mod_ref, g_ref, w_ref, wabt_ref,
                   mla_ref, rg_ref, gdn_ref, ab_ref, abt_ref, gates_ref):
    D = D_MODEL
    x = x_ref[...]
    sh = mod_ref[:, 0:D]
    sc = mod_ref[:, D:2 * D]
    h = (_rms(x, g_ref[...]) * (1.0 + sc) + sh).astype(BF16)
    o = _IN_OFFS
    mla_ref[...] = _dot(h, w_ref[:, o[0]:o[1]]).astype(BF16)
    rg_ref[...] = _dot(h, w_ref[:, o[1]:o[2]]).astype(BF16)
    gdn_ref[...] = _dot(h, w_ref[:, o[2]:o[3]]).astype(BF16)
    ab_ref[...] = _dot(h, w_ref[:, o[3]:o[4]])
    gates_ref[...] = _dot(h, w_ref[:, o[4]:o[5]]).astype(BF16)
    abt_ref[...] = _dot_nt(wabt_ref[...], h)


def _in_projection(x2d, mod, ln_g, w1, wabt, S, tm):
    T, D = x2d.shape
    per_seq = S // tm
    row = lambda i: (i, 0)
    outs = [(W_MLA, BF16), (W_RG, BF16), (W_GDN, BF16), (W_AB, F32)]
    out_shape = [jax.ShapeDtypeStruct((T, w), dt) for w, dt in outs]
    out_specs = [pl.BlockSpec((tm, w), row) for w, _ in outs]
    out_shape += [jax.ShapeDtypeStruct((16, T), F32), jax.ShapeDtypeStruct((T, W_GATES), BF16)]
    out_specs += [pl.BlockSpec((16, tm), lambda i: (0, i)), pl.BlockSpec((tm, W_GATES), row)]
    return pl.pallas_call(
        _inproj_kernel,
        grid=(T // tm,),
        in_specs=[pl.BlockSpec((tm, D), row),
                  pl.BlockSpec((None, 1, 6 * D), lambda i: (i // per_seq, 0, 0)),
                  _const_spec((1, D)),
                  _const_spec(w1.shape),
                  _const_spec(wabt.shape)],
        out_specs=out_specs,
        out_shape=out_shape,
        compiler_params=_cparams("arbitrary"),
        name="in_projection",
    )(x2d, mod, ln_g, w1, wabt)


def _mla_kernel(in_ref, csq_ref, csk_ref, gq_ref, wq_ref, gkv_ref, wk_ref, wv_ref,
                o_ref, q_s, k_s, v_s, *, S, tq, tr):
    hp = pl.program_id(1)

    @pl.when(hp == 0)
    def _project():
        for r in range(S // tr):
            rows = slice(r * tr, (r + 1) * tr)
            qd = in_ref[rows, 0:MLA_Q_LORA].astype(F32)
            ckv = in_ref[rows, MLA_Q_LORA:MLA_Q_LORA + MLA_KV_LORA].astype(F32)
            kr = in_ref[rows, 640:768].astype(F32) * csk_ref[rows, :]
            cq = _rms(qd, gq_ref[...]).astype(BF16)
            ckvn = _rms(ckv, gkv_ref[...]).astype(BF16)
            q = _dot(cq, wq_ref[...])
            kin = jnp.concatenate([ckvn, kr.astype(BF16)], axis=1)
            k = _dot(kin, wk_ref[...])
            v = _dot(ckvn, wv_ref[...])
            cs = csq_ref[rows, :]
            for p in range(MLA_HEADS // 2):
                for hh in range(2):
                    h = 2 * p + hh
                    q_s[p, rows, hh * LANE:(hh + 1) * LANE] = (
                        q[:, h * LANE:(h + 1) * LANE] * cs).astype(BF16)
                k_s[p, rows, :] = k[:, 2 * p * LANE:(2 * p + 2) * LANE].astype(BF16)
                v_s[p, rows, :] = v[:, 2 * p * LANE:(2 * p + 2) * LANE].astype(BF16)

    for qt in range(S // tq):
        rows = slice(qt * tq, (qt + 1) * tq)
        acc = None
        for hh in range(2):
            lanes = slice(hh * LANE, (hh + 1) * LANE)
            q = q_s[hp, rows, lanes]
            k = k_s[hp, :, lanes]
            s = _dot_nt(q, k)
            m = jnp.max(s, axis=-1, keepdims=True)
            p = jnp.exp(s - m)
            l = jnp.sum(p, axis=-1, keepdims=True)
            o = _dot(p.astype(BF16), v_s[hp, :, lanes]) * (1.0 / l)
            acc = o if acc is None else acc + o
        o_ref[rows, :] = acc.astype(BF16)


def _mla(mla_in, csq, csk, gq, wq, gkv, wk, wv, B, S):
    tq = min(512, S)
    tr = min(512, S)
    npair = MLA_HEADS // 2
    kern = functools.partial(_mla_kernel, S=S, tq=tq, tr=tr)
    return pl.pallas_call(
        kern,
        grid=(B, npair),
        in_specs=[pl.BlockSpec((S, W_MLA), lambda b, p: (b, 0)),
                  _const_spec(csq.shape), _const_spec(csk.shape),
                  _const_spec(gq.shape), _const_spec(wq.shape),
                  _const_spec(gkv.shape), _const_spec(wk.shape), _const_spec(wv.shape)],
        out_specs=pl.BlockSpec((S, LANE), lambda b, p: (b, p)),
        out_shape=jax.ShapeDtypeStruct((B * S, MLA_HEADS * MLA_V), BF16),
        scratch_shapes=[pltpu.VMEM((npair, S, 2 * LANE), BF16),
                        pltpu.VMEM((npair, S, 2 * LANE), BF16),
                        pltpu.VMEM((npair, S, 2 * LANE), BF16)],
        compiler_params=_cparams("arbitrary", "arbitrary"),
        name="mla_attention",
    )(mla_in, csq, csk, gq, wq, gkv, wk, wv)


def _conv4_window(xw, w_ref, R):
    acc = xw[7:7 + R] * w_ref[0:1, :]
    acc = acc + xw[8:8 + R] * w_ref[1:2, :]
    acc = acc + xw[9:9 + R] * w_ref[2:3, :]
    acc = acc + xw[10:10 + R] * w_ref[3:4, :]
    return acc


def _scan_rows(a, b, reverse):
    R = a.shape[0]
    row = lax.broadcasted_iota(jnp.int32, a.shape, 0)
    d = 1
    while d < R:
        if reverse:
            a_sh = pltpu.roll(a, R - d, 0)
            b_sh = pltpu.roll(b, R - d, 0)
            ok = row < R - d
        else:
            a_sh = pltpu.roll(a, d, 0)
            b_sh = pltpu.roll(b, d, 0)
            ok = row >= d
        b = b + a * jnp.where(ok, b_sh, 0.0)
        a = a * jnp.where(ok, a_sh, 1.0)
        d *= 2
    return a, b


def _rglru_kernel(in_ref, cw_ref, cb_ref, wg_ref, bg_ref, lam_ref, o_ref,
                  xpad, hf_s, carry_s, *, S, R):
    W = RG_WIDTH
    nchunk = S // R
    xpad[0:8, :] = jnp.zeros((8, W), F32)
    xpad[S + 8:S + 16, :] = jnp.zeros((8, W), F32)
    xpad[8:S + 8, :] = in_ref[:, 0:W].astype(F32)
    nsp = -RG_C * _softplus(-lam_ref[...])

    def direction(d, n):
        r0 = pl.multiple_of(n * R, R)
        xw = xpad[pl.ds(r0, R + 16), :]
        xc = _conv4_window(xw, cw_ref, R) + cb_ref[...]
        gates = _dot(xc.astype(BF16), wg_ref[:, 2 * d * W:(2 * d + 2) * W]) \
            + bg_ref[:, 2 * d * W:(2 * d + 2) * W]
        r = _sigmoid(gates[:, :W])
        i = _sigmoid(gates[:, W:])
        log_a = nsp[d:d + 1, :] * r
        a = jnp.exp(log_a)
        th = jnp.tanh(log_a)
        mult = jnp.sqrt(jnp.maximum(-2.0 * th / (1.0 - th), 0.0))
        t = r0 + lax.broadcasted_iota(jnp.int32, (R, W), 0)
        first = t == (S - 1 if d == 1 else 0)
        mult = jnp.where(first, 1.0, mult)
        acum, h = _scan_rows(a, mult * (i * xc), reverse=(d == 1))
        h = h + acum * carry_s[...]
        carry_s[...] = h[0:1, :] if d == 1 else h[R - 1:R, :]
        return r0, h

    carry_s[...] = jnp.zeros((1, W), F32)

    def fwd(n, c):
        r0, h = direction(0, n)
        hf_s[pl.ds(r0, R), :] = h
        return c

    lax.fori_loop(0, nchunk, fwd, 0)
    carry_s[...] = jnp.zeros((1, W), F32)

    def bwd(n, c):
        r0, h = direction(1, nchunk - 1 - n)
        gate = in_ref[pl.ds(r0, R), W:2 * W].astype(F32)
        o_ref[pl.ds(r0, R), :] = ((hf_s[pl.ds(r0, R), :] + h) * _gelu_tanh(gate)).astype(BF16)
        return c

    lax.fori_loop(0, nchunk, bwd, 0)


def _rglru(rg_in, cw, cb, wg, bg, lam, B, S):
    R = min(256, S)
    kern = functools.partial(_rglru_kernel, S=S, R=R)
    return pl.pallas_call(
        kern,
        grid=(B,),
        in_specs=[pl.BlockSpec((S, W_RG), lambda b: (b, 0)),
                  _const_spec(cw.shape), _const_spec(cb.shape), _const_spec(wg.shape),
                  _const_spec(bg.shape), _const_spec(lam.shape)],
        out_specs=pl.BlockSpec((S, RG_WIDTH), lambda b: (b, 0)),
        out_shape=jax.ShapeDtypeStruct((B * S, RG_WIDTH), BF16),
        scratch_shapes=[pltpu.VMEM((S + 16, RG_WIDTH), F32),
                        pltpu.VMEM((S, RG_WIDTH), F32),
                        pltpu.VMEM((1, RG_WIDTH), F32)],
        compiler_params=_cparams("arbitrary"),
        name="rglru",
    )(rg_in, cw, cb, wg, bg, lam)


def _tri_consts():
    C = GDN_CHUNK
    r = lax.broadcasted_iota(jnp.int32, (2 * C, 2 * C), 0)
    c = lax.broadcasted_iota(jnp.int32, (2 * C, 2 * C), 1)
    same = (r // C) == (c // C)
    one = lambda m: jnp.where(m, 1.0, 0.0).astype(BF16)
    return r, c, same, one


def _unit_tri_inverse(a, ri, ci, eye):
    blk = lambda n: (ri // n) == (ci // n)
    p = jnp.where(blk(8), -a, 0.0).astype(BF16)
    t = jnp.where(eye, 1.0, 0.0) + p.astype(F32)
    for _ in range(2):
        p = _dot(p, p).astype(BF16)
        t = t + _dot(t.astype(BF16), p)
    n = 8
    while n < a.shape[0]:
        off = jnp.where(blk(2 * n) & jnp.logical_not(blk(n)), a, 0.0).astype(BF16)
        tb = t.astype(BF16)
        t = t - _dot(_dot(tb, off).astype(BF16), tb)
        n *= 2
    return t


def _gdn_kernel(in_ref, ab_ref, abt_ref, cw_ref, alr_ref, dtr_ref, alc_ref, dtc_ref, ng_ref,
                o_ref, xpad, qn_s, kn_s, vn_s, gcol_s, bcol_s, grow_s, rrow_s, brow_s,
                m1_s, m2_s, st_s, o_s, *, S):
    C = GDN_CHUNK
    H = GDN_HEADS
    NC = S // C
    HW = H * GDN_DK
    R = min(256, S)

    xpad[0:8, :] = jnp.zeros((8, HW), F32)
    xpad[S + 8:S + 16, :] = jnp.zeros((8, HW), F32)
    for grp, dst in enumerate((qn_s, kn_s, vn_s)):
        xpad[8:S + 8, :] = in_ref[:, grp * HW:(grp + 1) * HW].astype(F32)

        def conv_body(n, c, grp=grp, dst=dst):
            r0 = pl.multiple_of(n * R, R)
            xw = xpad[pl.ds(r0, R + 16), :]
            y = _silu(_conv4_window(xw, cw_ref.at[:, grp * HW:(grp + 1) * HW], R))
            if grp < 2:
                for h in range(H):
                    t = y[:, h * LANE:(h + 1) * LANE]
                    t = t * lax.rsqrt(jnp.sum(t * t, axis=-1, keepdims=True) + EPS)
                    dst[pl.ds(r0, R), h * LANE:(h + 1) * LANE] = t.astype(BF16)
            else:
                dst[pl.ds(r0, R), :] = y.astype(BF16)
            return c

        lax.fori_loop(0, S // R, conv_body, 0)

    r, c, same, one = _tri_consts()
    lane = lax.broadcasted_iota(jnp.int32, (2 * C, LANE), 1)
    is_fwd_lane = lane < H
    sub = lax.broadcasted_iota(jnp.int32, (N_STREAMS, LANE), 0)
    is_fwd_sub = sub < H
    tl_incl = one(same & (c <= r))
    tu_incl = one(same & (c >= r))
    lanec = lax.broadcasted_iota(jnp.int32, (2 * C, LANE), 1)

    def sum3(parts, mat, left):
        acc = None
        for p_ in parts:
            t = _dot(mat, p_) if left else _dot(p_, mat)
            acc = t if acc is None else acc + t
        return acc

    for pr in range(S // (2 * C)):
        rows = slice(pr * 2 * C, (pr + 1) * 2 * C)
        ab = ab_ref[rows, :]
        g_col = -jnp.exp(alr_ref[...]) * _softplus(ab + dtr_ref[...])
        parts = _split3(g_col)
        gcol_s[rows, :] = jnp.where(is_fwd_lane, sum3(parts, tl_incl, True), sum3(parts, tu_incl, True))
        bcol_s[rows, :] = _sigmoid(ab)
        abt = abt_ref[:, rows]
        g_row = -jnp.exp(alc_ref[...]) * _softplus(abt[0:N_STREAMS, :] + dtc_ref[...])
        b_row = _sigmoid(abt[N_STREAMS:2 * N_STREAMS, :])
        parts = _split3(g_row)
        for half in range(2):
            in_half = (r // C) == half
            cj = c % C
            rj = r % C
            incl_f = one(in_half & (rj <= cj))
            incl_b = one(in_half & (rj >= cj))
            excl_f = one(in_half & (rj > cj))
            excl_b = one(in_half & (rj < cj))
            ch = pr * 2 + half
            grow_s[ch] = jnp.where(is_fwd_sub, sum3(parts, incl_f, False), sum3(parts, incl_b, False))
            rrow_s[ch] = jnp.where(is_fwd_sub, sum3(parts, excl_f, False), sum3(parts, excl_b, False))
            dup = one(in_half & (rj == cj))
            brow_s[ch] = _dot(b_row.astype(BF16), dup) + _dot((b_row - b_row.astype(BF16).astype(F32)).astype(BF16), dup)

    ri = lax.broadcasted_iota(jnp.int32, (C, C), 0)
    ci = lax.broadcasted_iota(jnp.int32, (C, C), 1)
    eye = ri == ci
    scale = GDN_DK ** -0.5

    def phase1(n, carry):
        r0 = pl.multiple_of(n * C, C)
        gcol = gcol_s[pl.ds(r0, C), :]
        bcol = bcol_s[pl.ds(r0, C), :]
        grow = grow_s[n]
        rrow = rrow_s[n]
        brow = brow_s[n]
        for h in range(H):
            kc = kn_s[pl.ds(r0, C), h * LANE:(h + 1) * LANE]
            qc = qn_s[pl.ds(r0, C), h * LANE:(h + 1) * LANE]
            kq = _dot_nt(jnp.concatenate([kc, qc], axis=0), kc)
            kk = kq[0:C]
            qk = kq[C:2 * C]
            for d in range(2):
                s = d * H + h
                gi = gcol[:, s:s + 1]
                gj = grow[s:s + 1, 0:C]
                bi = bcol[:, N_STREAMS + s:N_STREAMS + s + 1]
                bj = brow[s:s + 1, 0:C]
                incl = (ci <= ri) if d == 0 else (ci >= ri)
                strict = (ci < ri) if d == 0 else (ci > ri)
                decay = jnp.where(incl, jnp.exp(jnp.minimum(gi - gj, 0.0)), 0.0)
                a = jnp.where(strict, bi * kk * decay, 0.0)
                t = _unit_tri_inverse(a, ri, ci, eye)
                egj = jnp.exp(gj)
                tb = t * bj
                m1 = jnp.concatenate([tb, -(tb * egj)], axis=1)
                attn = qk * decay * scale
                dg = jnp.where(eye, egj * scale, 0.0)
                dk = jnp.where(eye, jnp.exp(rrow[s:s + 1, 0:C]), 0.0)
                top = jnp.concatenate([attn, dg], axis=1)
                bot = jnp.concatenate([dk, jnp.zeros((C, C), F32)], axis=1)
                m1_s[n * N_STREAMS + s] = m1.astype(BF16)
                m2_s[n * N_STREAMS + s] = jnp.concatenate([top, bot], axis=0).astype(BF16)
        return carry

    lax.fori_loop(0, NC, phase1, 0)

    st_s[...] = jnp.zeros(st_s.shape, F32)
    o_s[...] = jnp.zeros(o_s.shape, F32)

    def phase2(n, carry):
        for d in range(2):
            ch = n if d == 0 else NC - 1 - n
            r0 = pl.multiple_of(ch * C, C)
            etot_all = jnp.exp(grow_s[ch] + rrow_s[ch])
            for h in range(H):
                s = d * H + h
                lanes = slice(h * LANE, (h + 1) * LANE)
                kc = kn_s[pl.ds(r0, C), lanes]
                qc = qn_s[pl.ds(r0, C), lanes]
                vc = vn_s[pl.ds(r0, C), lanes]
                st = st_s[s]
                kqs = _dot(jnp.concatenate([kc, qc], axis=0), st.astype(BF16))
                r1 = jnp.concatenate([vc, kqs[0:C].astype(BF16)], axis=0)
                v_new = _dot(m1_s[ch * N_STREAMS + s], r1)
                r2 = jnp.concatenate([v_new.astype(BF16), kqs[C:2 * C].astype(BF16)], axis=0)
                ox = _dot(m2_s[ch * N_STREAMS + s], r2)
                st_s[s] = st * etot_all[s:s + 1, :] + _dot_tn(kc, ox[C:2 * C].astype(BF16))
                o_s[pl.ds(r0, C), lanes] += ox[0:C]
        return carry

    lax.fori_loop(0, NC, phase2, 0)

    def phase3(n, carry):
        r0 = pl.multiple_of(n * R, R)
        z = in_ref[pl.ds(r0, R), 3 * HW:4 * HW].astype(F32)
        for h in range(H):
            lanes = slice(h * LANE, (h + 1) * LANE)
            o = o_s[pl.ds(r0, R), lanes]
            o = _rms(o, ng_ref[...]) * _silu(z[:, lanes])
            o_ref[pl.ds(r0, R), lanes] = o.astype(BF16)
        return carry

    lax.fori_loop(0, S // R, phase3, 0)


def _gdn(gdn_in, ab, abt, cw, alr, dtr, alc, dtc, ng, B, S):
    C = GDN_CHUNK
    NC = S // C
    HW = GDN_HEADS * GDN_DK
    kern = functools.partial(_gdn_kernel, S=S)
    scratch = [pltpu.VMEM((S + 16, HW), F32),
               pltpu.VMEM((S, HW), BF16), pltpu.VMEM((S, HW), BF16), pltpu.VMEM((S, HW), BF16),
               pltpu.VMEM((S, LANE), F32), pltpu.VMEM((S, LANE), F32),
               pltpu.VMEM((NC, N_STREAMS, LANE), F32), pltpu.VMEM((NC, N_STREAMS, LANE), F32),
               pltpu.VMEM((NC, N_STREAMS, LANE), F32),
               pltpu.VMEM((NC * N_STREAMS, C, 2 * C), BF16),
               pltpu.VMEM((NC * N_STREAMS, 2 * C, 2 * C), BF16),
               pltpu.VMEM((N_STREAMS, GDN_DK, GDN_DV), F32),
               pltpu.VMEM((S, HW), F32)]
    return pl.pallas_call(
        kern,
        grid=(B,),
        in_specs=[pl.BlockSpec((S, W_GDN), lambda b: (b, 0), pipeline_mode=pl.Buffered(1)),
                  pl.BlockSpec((S, W_AB), lambda b: (b, 0)),
                  pl.BlockSpec((16, S), lambda b: (0, b)),
                  _const_spec(cw.shape), _const_spec(alr.shape), _const_spec(dtr.shape),
                  _const_spec(alc.shape), _const_spec(dtc.shape), _const_spec(ng.shape)],
        out_specs=pl.BlockSpec((S, HW), lambda b: (b, 0)),
        out_shape=jax.ShapeDtypeStruct((B * S, HW), BF16),
        scratch_shapes=scratch,
        compiler_params=_cparams("arbitrary"),
        name="gated_deltanet",
    )(gdn_in, ab, abt, cw, alr, dtr, alc, dtc, ng)


def _merge_kernel(x_ref, mod_ref, om_ref, or_ref, og_ref, gates_ref, wb_ref, wo_ref, o_ref):
    D = D_MODEL
    mixed = None
    for n, src in enumerate((om_ref, or_ref, og_ref)):
        g = _sigmoid(gates_ref[:, n * D:(n + 1) * D].astype(F32))
        term = g * _dot(src[...], wb_ref[n])
        mixed = term if mixed is None else mixed + term
    gt1 = mod_ref[:, 2 * D:3 * D]
    o_ref[...] = x_ref[...] + gt1 * _dot(mixed.astype(BF16), wo_ref[...])


def _merge(x2d, mod, o_mla, o_rg, o_gdn, gates, wb, wo, S, tm):
    T, D = x2d.shape
    per_seq = S // tm
    row = lambda i: (i, 0)
    return pl.pallas_call(
        _merge_kernel,
        grid=(T // tm,),
        in_specs=[pl.BlockSpec((tm, D), row),
                  pl.BlockSpec((None, 1, 6 * D), lambda i: (i // per_seq, 0, 0)),
                  pl.BlockSpec((tm, 512), row), pl.BlockSpec((tm, 512), row), pl.BlockSpec((tm, 512), row),
                  pl.BlockSpec((tm, W_GATES), row),
                  _const_spec(wb.shape), _const_spec(wo.shape)],
        out_specs=pl.BlockSpec((tm, D), row),
        out_shape=jax.ShapeDtypeStruct((T, D), F32),
        compiler_params=_cparams("arbitrary"),
        name="branch_merge",
    )(x2d, mod, o_mla, o_rg, o_gdn, gates, wb, wo)


def _ffn_kernel(x_ref, xp_ref, xn_ref, mod_ref, g_ref, wu_ref, cw_ref, cb_ref, wd_ref, fg_ref,
                o_ref, acc_s, *, tm, per_seq, final):
    D = D_MODEL
    i = pl.program_id(0)
    sh = mod_ref[:, 3 * D:4 * D]
    sc = mod_ref[:, 4 * D:5 * D]
    gt2 = mod_ref[:, 5 * D:6 * D]
    x = x_ref[...]
    h = (_rms(x, g_ref[...]) * (1.0 + sc) + sh).astype(BF16)
    xh = jnp.concatenate([xp_ref[...], xn_ref[...]], axis=0)
    hh = (_rms(xh, g_ref[...]) * (1.0 + sc) + sh).astype(BF16)
    has_prev = jnp.where(i % per_seq == 0, 0.0, 1.0)
    has_next = jnp.where(i % per_seq == per_seq - 1, 0.0, 1.0)
    row = lax.broadcasted_iota(jnp.int32, (tm, 2 * FFN_CK), 0)
    acc_s[...] = jnp.zeros((tm, D), F32)

    def body(j, c):
        w = wu_ref[j]
        up = _dot(h, w)
        uh = _dot(hh, w)
        prev = uh[7:8, :] * has_prev
        nxt = uh[8:9, :] * has_next
        dn = jnp.where(row == 0, prev, pltpu.roll(up, 1, 0))
        un = jnp.where(row == tm - 1, nxt, pltpu.roll(up, tm - 1, 0))
        cw = cw_ref[j]
        y = dn * cw[0:1, :] + up * cw[1:2, :] + un * cw[2:3, :] + cb_ref[j]
        act = _silu(y[:, FFN_CK:]) * y[:, :FFN_CK]
        acc_s[...] += _dot(act.astype(BF16), wd_ref[j])
        return c

    lax.fori_loop(0, D_FF // FFN_CK, body, 0)
    y = x + gt2 * acc_s[...]
    if final:
        y = _rms(y, fg_ref[...])
    o_ref[...] = y


def _ffn(x2d, mod, ln_g, wu, cw, cb, wd, fg, S, tm, final):
    T, D = x2d.shape
    per_seq = S // tm
    hb = tm // 8
    nblk8 = T // 8
    kern = functools.partial(_ffn_kernel, tm=tm, per_seq=per_seq, final=final)
    return pl.pallas_call(
        kern,
        grid=(T // tm,),
        in_specs=[pl.BlockSpec((tm, D), lambda i: (i, 0)),
                  pl.BlockSpec((8, D), lambda i: (jnp.maximum(i * hb - 1, 0), 0)),
                  pl.BlockSpec((8, D), lambda i: (jnp.minimum((i + 1) * hb, nblk8 - 1), 0)),
                  pl.BlockSpec((None, 1, 6 * D), lambda i: (i // per_seq, 0, 0)),
                  _const_spec((1, D)),
                  _const_spec(wu.shape), _const_spec(cw.shape), _const_spec(cb.shape),
                  _const_spec(wd.shape), _const_spec((1, D))],
        out_specs=pl.BlockSpec((tm, D), lambda i: (i, 0)),
        out_shape=jax.ShapeDtypeStruct((T, D), F32),
        scratch_shapes=[pltpu.VMEM((tm, D), F32)],
        compiler_params=_cparams("arbitrary"),
        name="conv_glu_ffn",
    )(x2d, x2d, x2d, mod, ln_g, wu, cw, cb, wd, fg)


def _rot_cols(w):
    half = w.shape[-1] // 2
    return jnp.concatenate([-w[..., half:], w[..., :half]], axis=-1)


def _prep_layer(l, p):
    D = D_MODEL
    w = p['w_in'][l]
    o = np.cumsum([0, MLA_Q_LORA, MLA_KV_LORA + MLA_ROPE, RG_WIDTH, RG_WIDTH, 3 * 512, 512, 8, 8, 3 * D])
    kro = o[1] + MLA_KV_LORA
    w_kr = w[:, kro:o[2]]
    w1 = jnp.concatenate([
        w[:, o[0]:o[1]], w[:, o[1]:kro], w_kr, _rot_cols(w_kr), jnp.zeros((D, 64), F32),
        w[:, o[2]:o[4]],
        w[:, o[4]:o[6]],
        w[:, o[6]:o[8]], jnp.zeros((D, W_AB - 16), F32),
        w[:, o[8]:o[9]]], axis=1).astype(BF16)
    wabt = w[:, o[6]:o[8]].T.astype(BF16)

    H = MLA_HEADS
    wq = p['mla_w_uq'][l].reshape(MLA_Q_LORA, H, MLA_QK)
    rope = wq[..., MLA_NOPE:]
    wq_ext = jnp.concatenate([wq[..., :MLA_NOPE], rope, _rot_cols(rope)], axis=-1)
    wq_ext = wq_ext.reshape(MLA_Q_LORA, H * HEAD_PAD).astype(BF16)
    wkv = p['mla_w_ukv'][l].reshape(MLA_KV_LORA, H, MLA_NOPE + MLA_V)
    wk_top = jnp.concatenate([wkv[..., :MLA_NOPE], jnp.zeros((MLA_KV_LORA, H, 64), F32)], axis=-1)
    e = np.zeros((128, H, HEAD_PAD), np.float32)
    for j in range(MLA_ROPE):
        for src in (j, MLA_ROPE + j):
            e[src, :, MLA_NOPE + j] = 1.0
            e[src, :, MLA_NOPE + MLA_ROPE + j] = 1.0
    wk_ext = jnp.concatenate([wk_top, jnp.asarray(e)], axis=0).reshape(MLA_KV_LORA + 128, H * HEAD_PAD).astype(BF16)
    wv = wkv[..., MLA_NOPE:]
    z = jnp.zeros_like(wv)
    even = (np.arange(H) % 2 == 0)[None, :, None]
    wv_ext = jnp.concatenate([jnp.where(even, wv, z), jnp.where(even, z, wv)], axis=-1)
    wv_ext = wv_ext.reshape(MLA_KV_LORA, H * HEAD_PAD).astype(BF16)

    eye = jnp.eye(RG_BLOCKS, dtype=F32)
    bd = lambda t: jnp.einsum('ncd,nm->ncmd', t, eye).reshape(RG_WIDTH, RG_WIDTH)
    wg = jnp.concatenate([bd(p['rg_w_a'][l, 0]), bd(p['rg_w_i'][l, 0]),
                          bd(p['rg_w_a'][l, 1]), bd(p['rg_w_i'][l, 1])], axis=1).astype(BF16)
    bg = jnp.concatenate([p['rg_b_a'][l, 0], p['rg_b_i'][l, 0], p['rg_b_a'][l, 1], p['rg_b_i'][l, 1]])[None, :]

    pad_row = lambda v: jnp.concatenate([v.reshape(-1), jnp.zeros((LANE - N_STREAMS,), F32)])[None, :]
    bc_col = lambda v: jnp.broadcast_to(v.reshape(-1, 1), (N_STREAMS, LANE))

    nck = D_FF // FFN_CK
    pair = lambda t: jnp.concatenate([t[..., :D_FF].reshape(t.shape[:-1] + (nck, FFN_CK)),
                                      t[..., D_FF:].reshape(t.shape[:-1] + (nck, FFN_CK))], axis=-1)
    wu = jnp.moveaxis(pair(p['ffn_w_up'][l]), 1, 0).astype(BF16)
    fcw = jnp.moveaxis(pair(p['ffn_conv_w'][l]), 1, 0)
    fcb = pair(p['ffn_conv_b'][l])[:, None, :]
    wd = p['ffn_w_down'][l].reshape(nck, FFN_CK, D).astype(BF16)

    return dict(
        ln1_g=p['ln1_g'][l][None, :], w1=w1, wabt=wabt,
        gq=p['mla_q_norm_g'][l][None, :], wq=wq_ext, gkv=p['mla_kv_norm_g'][l][None, :],
        wk=wk_ext, wv=wv_ext,
        rg_cw=p['rg_conv_w'][l], rg_cb=p['rg_conv_b'][l][None, :], wg=wg, bg=bg, lam=p['rg_lam'][l],
        gdn_cw=p['gdn_conv_w'][l], alr=pad_row(p['gdn_a_log'][l]), dtr=pad_row(p['gdn_dt_bias'][l]),
        alc=bc_col(p['gdn_a_log'][l]), dtc=bc_col(p['gdn_dt_bias'][l]), ng=p['gdn_norm_g'][l][None, :],
        wb=p['w_branch'][l].astype(BF16), wo=p['w_out'][l].astype(BF16),
        ln2_g=p['ln2_g'][l][None, :], wu=wu, fcw=fcw, fcb=fcb, wd=wd)


def _rope_tables(S):
    inv = 1.0 / (ROPE_THETA ** (jnp.arange(0, MLA_ROPE, 2, dtype=F32) / MLA_ROPE))
    ang = jnp.arange(S, dtype=F32)[:, None] * inv[None, :]
    cos, sin = jnp.cos(ang), jnp.sin(ang)
    scale = MLA_QK ** -0.5
    csq = scale * jnp.concatenate([jnp.ones((S, MLA_NOPE), F32), cos, cos, sin, sin], axis=1)
    csk = jnp.concatenate([cos, cos, sin, sin, jnp.zeros((S, 64), F32)], axis=1)
    return csq, csk


def _trunk(x, mod, layers, final_g, csq, csk):
    B, S, D = x.shape
    x2d = x.reshape(B * S, D)
    tm = min(256, S)
    for l, w in enumerate(layers):
        m = mod[l].reshape(B, 1, 6 * D)
        mla_in, rg_in, gdn_in, ab, abt, gates = _in_projection(x2d, m, w['ln1_g'], w['w1'], w['wabt'], S, tm)
        o_mla = _mla(mla_in, csq, csk, w['gq'], w['wq'], w['gkv'], w['wk'], w['wv'], B, S)
        o_rg = _rglru(rg_in, w['rg_cw'], w['rg_cb'], w['wg'], w['bg'], w['lam'], B, S)
        o_gdn = _gdn(gdn_in, ab, abt, w['gdn_cw'], w['alr'], w['dtr'], w['alc'], w['dtc'], w['ng'], B, S)
        x2d = _merge(x2d, m, o_mla, o_rg, o_gdn, gates, w['wb'], w['wo'], S, tm)
        x2d = _ffn(x2d, m, w['ln2_g'], w['wu'], w['fcw'], w['fcb'], w['wd'], final_g, S, tm,
                   final=(l == len(layers) - 1))
    return x2d.reshape(B, S, D)


def kernel(x_prompt, x_sample, c_prompt, c_sample, ln1_g, w_mod, b_mod, w_in, mla_q_norm_g, mla_w_uq, mla_kv_norm_g, mla_w_ukv, rg_conv_w, rg_conv_b, rg_w_a, rg_b_a, rg_w_i, rg_b_i, rg_lam, gdn_conv_w, gdn_a_log, gdn_dt_bias, gdn_norm_g, w_branch, w_out, ln2_g, ffn_w_up, ffn_conv_w, ffn_conv_b, ffn_w_down, final_norm_g):
    p = dict(ln1_g=ln1_g, w_in=w_in, mla_q_norm_g=mla_q_norm_g, mla_w_uq=mla_w_uq,
             mla_kv_norm_g=mla_kv_norm_g, mla_w_ukv=mla_w_ukv, rg_conv_w=rg_conv_w, rg_conv_b=rg_conv_b,
             rg_w_a=rg_w_a, rg_b_a=rg_b_a, rg_w_i=rg_w_i, rg_b_i=rg_b_i, rg_lam=rg_lam,
             gdn_conv_w=gdn_conv_w, gdn_a_log=gdn_a_log, gdn_dt_bias=gdn_dt_bias, gdn_norm_g=gdn_norm_g,
             w_branch=w_branch, w_out=w_out, ln2_g=ln2_g, ffn_w_up=ffn_w_up, ffn_conv_w=ffn_conv_w,
             ffn_conv_b=ffn_conv_b, ffn_w_down=ffn_w_down)
    L = w_in.shape[0]
    layers = [_prep_layer(l, p) for l in range(L)]
    Bp = x_prompt.shape[0]
    mod = _modulation(jnp.concatenate([c_prompt, c_sample], axis=0), w_mod, b_mod)
    fg = final_norm_g[None, :]
    outs = []
    for x, m in ((x_prompt, mod[:, :Bp]), (x_sample, mod[:, Bp:])):
        csq, csk = _rope_tables(x.shape[1])
        outs.append(_trunk(x, m, layers, fg, csq, csk))
    return tuple(outs)
```

```python
import functools
import math

import jax
import jax.numpy as jnp
import numpy as np
from jax import lax
from jax.experimental import pallas as pl
from jax.experimental.pallas import tpu as pltpu

D_MODEL = 1024
DEPTH = 2
MLA_HEADS = 8
MLA_Q_LORA = 384
MLA_KV_LORA = 256
MLA_NOPE = 64
MLA_ROPE = 32
MLA_V = 64
MLA_QK = MLA_NOPE + MLA_ROPE
ROPE_THETA = 10000.0
RG_WIDTH = 512
RG_BLOCKS = 8
RG_C = 8.0
GDN_HEADS = 4
GDN_DK = 128
GDN_DV = 128
GDN_CHUNK = 64
D_FF = 2816
EPS = 1e-6

LANE = 128
HEAD_PAD = 128
VMEM_LIMIT = 56 * 1024 * 1024
FFN_CK = 256
N_STREAMS = 2 * GDN_HEADS

F32 = jnp.float32
BF16 = jnp.bfloat16


def _cparams(*sem):
    return pltpu.CompilerParams(dimension_semantics=sem, vmem_limit_bytes=VMEM_LIMIT)


def _const_spec(shape):
    nd = len(shape)
    return pl.BlockSpec(shape, lambda *_: (0,) * nd, pipeline_mode=pl.Buffered(1))


def _sigmoid(x):
    return 1.0 / (1.0 + jnp.exp(-x))


def _silu(x):
    return x * _sigmoid(x)


def _softplus(x):
    return jnp.maximum(x, 0.0) + jnp.log(1.0 + jnp.exp(-jnp.abs(x)))


def _gelu_tanh(x):
    c = math.sqrt(2.0 / math.pi)
    return 0.5 * x * (1.0 + jnp.tanh(c * (x + 0.044715 * (x * x * x))))


def _rms(x, g):
    return x * lax.rsqrt(jnp.mean(x * x, axis=-1, keepdims=True) + EPS) * g


def _dot(a, b):
    return jnp.dot(a, b, preferred_element_type=F32)


def _dot_nt(a, b):
    return lax.dot_general(a, b, (((1,), (1,)), ((), ())), preferred_element_type=F32)


def _dot_tn(a, b):
    return lax.dot_general(a, b, (((0,), (0,)), ((), ())), preferred_element_type=F32)


def _split3(x):
    x1 = x.astype(BF16)
    r1 = x - x1.astype(F32)
    x2 = r1.astype(BF16)
    x3 = (r1 - x2.astype(F32)).astype(BF16)
    return x1, x2, x3


def _mod_kernel(c_ref, w_ref, b_ref, o_ref):
    c = c_ref[...]
    a1, a2, a3 = _split3(_silu(c))
    w1, w2, w3 = _split3(w_ref[...])
    acc = _dot(a1, w1) + (_dot(a1, w2) + _dot(a2, w1)) + (_dot(a1, w3) + _dot(a2, w2) + _dot(a3, w1))
    o_ref[...] = acc + b_ref[...]


def _modulation(c, w_mod, b_mod):
    L, D, N = w_mod.shape
    Bt = c.shape[0]
    tn = 1024
    return pl.pallas_call(
        _mod_kernel,
        grid=(L, N // tn),
        in_specs=[pl.BlockSpec((Bt, D), lambda l, j: (0, 0)),
                  pl.BlockSpec((None, D, tn), lambda l, j: (l, 0, j)),
                  pl.BlockSpec((None, 1, tn), lambda l, j: (l, 0, j))],
        out_specs=pl.BlockSpec((None, Bt, tn), lambda l, j: (l, 0, j)),
        out_shape=jax.ShapeDtypeStruct((L, Bt, N), F32),
        compiler_params=_cparams("arbitrary", "arbitrary"),
        name="modulation",
    )(c, w_mod, b_mod.reshape(L, 1, N))


W_MLA, W_RG, W_GDN, W_AB, W_GATES = 768, 1024, 2048, 128, 3 * D_MODEL
_IN_OFFS = np.cumsum([0, W_MLA, W_RG, W_GDN, W_AB, W_GATES])


def _inproj_kernel(x_ref, mod_ref, g_ref, w_ref, wabt_ref,
                   mla_ref, rg_ref, gdn_ref, ab_ref, abt_ref, gates_ref):
    D = D_MODEL
    x = x_ref[...]
    sh = mod_ref[:, 0:D]
    sc = mod_ref[:, D:2 * D]
    h = (_rms(x, g_ref[...]) * (1.0 + sc) + sh).astype(BF16)
    o = _IN_OFFS
    mla_ref[...] = _dot(h, w_ref[:, o[0]:o[1]]).astype(BF16)
    rg_ref[...] = _dot(h, w_ref[:, o[1]:o[2]]).astype(BF16)
    gdn_ref[...] = _dot(h, w_ref[:, o[2]:o[3]]).astype(BF16)
    ab_ref[...] = _dot(h, w_ref[:, o[3]:o[4]])
    gates_ref[...] = _dot(h, w_ref[:, o[4]:o[5]]).astype(BF16)
    abt_ref[...] = _dot_nt(wabt_ref[...], h)


def _in_projection(x2d, mod, ln_g, w1, wabt, S, tm):
    T, D = x2d.shape
    per_seq = S // tm
    row = lambda i: (i, 0)
    outs = [(W_MLA, BF16), (W_RG, BF16), (W_GDN, BF16), (W_AB, F32)]
    out_shape = [jax.ShapeDtypeStruct((T, w), dt) for w, dt in outs]
    out_specs = [pl.BlockSpec((tm, w), row) for w, _ in outs]
    out_shape += [jax.ShapeDtypeStruct((16, T), F32), jax.ShapeDtypeStruct((T, W_GATES), BF16)]
    out_specs += [pl.BlockSpec((16, tm), lambda i: (0, i)), pl.BlockSpec((tm, W_GATES), row)]
    return pl.pallas_call(
        _inproj_kernel,
        grid=(T // tm,),
        in_specs=[pl.BlockSpec((tm, D), row),
                  pl.BlockSpec((None, 1, 6 * D), lambda i: (i // per_seq, 0, 0)),
                  _const_spec((1, D)),
                  _const_spec(w1.shape),
                  _const_spec(wabt.shape)],
        out_specs=out_specs,
        out_shape=out_shape,
        compiler_params=_cparams("arbitrary"),
        name="in_projection",
    )(x2d, mod, ln_g, w1, wabt)


def _mla_kernel(in_ref, csq_ref, csk_ref, gq_ref, wq_ref, gkv_ref, wk_ref, wv_ref,
                o_ref, q_s, k_s, v_s, *, S, tq, tr):
    hp = pl.program_id(1)

    @pl.when(hp == 0)
    def _project():
        for r in range(S // tr):
            rows = slice(r * tr, (r + 1) * tr)
            qd = in_ref[rows, 0:MLA_Q_LORA].astype(F32)
            ckv = in_ref[rows, MLA_Q_LORA:MLA_Q_LORA + MLA_KV_LORA].astype(F32)
            kr = in_ref[rows, 640:768].astype(F32) * csk_ref[rows, :]
            cq = _rms(qd, gq_ref[...]).astype(BF16)
            ckvn = _rms(ckv, gkv_ref[...]).astype(BF16)
            q = _dot(cq, wq_ref[...])
            kin = jnp.concatenate([ckvn, kr.astype(BF16)], axis=1)
            k = _dot(kin, wk_ref[...])
            v = _dot(ckvn, wv_ref[...])
            cs = csq_ref[rows, :]
            for p in range(MLA_HEADS // 2):
                for hh in range(2):
                    h = 2 * p + hh
                    q_s[p, rows, hh * LANE:(hh + 1) * LANE] = (
                        q[:, h * LANE:(h + 1) * LANE] * cs).astype(BF16)
                k_s[p, rows, :] = k[:, 2 * p * LANE:(2 * p + 2) * LANE].astype(BF16)
                v_s[p, rows, :] = v[:, 2 * p * LANE:(2 * p + 2) * LANE].astype(BF16)

    for qt in range(S // tq):
        rows = slice(qt * tq, (qt + 1) * tq)
        acc = None
        for hh in range(2):
            lanes = slice(hh * LANE, (hh + 1) * LANE)
            q = q_s[hp, rows, lanes]
            k = k_s[hp, :, lanes]
            s = _dot_nt(q, k)
            m = jnp.max(s, axis=-1, keepdims=True)
            p = jnp.exp(s - m)
            l = jnp.sum(p, axis=-1, keepdims=True)
            o = _dot(p.astype(BF16), v_s[hp, :, lanes]) * (1.0 / l)
            acc = o if acc is None else acc + o
        o_ref[rows, :] = acc.astype(BF16)


def _mla(mla_in, csq, csk, gq, wq, gkv, wk, wv, B, S):
    tq = min(512, S)
    tr = min(512, S)
    npair = MLA_HEADS // 2
    kern = functools.partial(_mla_kernel, S=S, tq=tq, tr=tr)
    return pl.pallas_call(
        kern,
        grid=(B, npair),
        in_specs=[pl.BlockSpec((S, W_MLA), lambda b, p: (b, 0)),
                  _const_spec(csq.shape), _const_spec(csk.shape),
                  _const_spec(gq.shape), _const_spec(wq.shape),
                  _const_spec(gkv.shape), _const_spec(wk.shape), _const_spec(wv.shape)],
        out_specs=pl.BlockSpec((S, LANE), lambda b, p: (b, p)),
        out_shape=jax.ShapeDtypeStruct((B * S, MLA_HEADS * MLA_V), BF16),
        scratch_shapes=[pltpu.VMEM((npair, S, 2 * LANE), BF16),
                        pltpu.VMEM((npair, S, 2 * LANE), BF16),
                        pltpu.VMEM((npair, S, 2 * LANE), BF16)],
        compiler_params=_cparams("arbitrary", "arbitrary"),
        name="mla_attention",
    )(mla_in, csq, csk, gq, wq, gkv, wk, wv)


def _conv4_window(xw, w_ref, R):
    acc = xw[7:7 + R] * w_ref[0:1, :]
    acc = acc + xw[8:8 + R] * w_ref[1:2, :]
    acc = acc + xw[9:9 + R] * w_ref[2:3, :]
    acc = acc + xw[10:10 + R] * w_ref[3:4, :]
    return acc


def _scan_rows(a, b, reverse):
    R = a.shape[0]
    row = lax.broadcasted_iota(jnp.int32, a.shape, 0)
    d = 1
    while d < R:
        if reverse:
            a_sh = pltpu.roll(a, R - d, 0)
            b_sh = pltpu.roll(b, R - d, 0)
            ok = row < R - d
        else:
            a_sh = pltpu.roll(a, d, 0)
            b_sh = pltpu.roll(b, d, 0)
            ok = row >= d
        b = b + a * jnp.where(ok, b_sh, 0.0)
        a = a * jnp.where(ok, a_sh, 1.0)
        d *= 2
    return a, b


def _rglru_kernel(in_ref, cw_ref, cb_ref, wg_ref, bg_ref, lam_ref, o_ref,
                  xpad, hf_s, carry_s, *, S, R):
    W = RG_WIDTH
    nchunk = S // R
    xpad[0:8, :] = jnp.zeros((8, W), F32)
    xpad[S + 8:S + 16, :] = jnp.zeros((8, W), F32)
    xpad[8:S + 8, :] = in_ref[:, 0:W].astype(F32)
    nsp = -RG_C * _softplus(-lam_ref[...])

    def direction(d, n):
        r0 = pl.multiple_of(n * R, R)
        xw = xpad[pl.ds(r0, R + 16), :]
        xc = _conv4_window(xw, cw_ref, R) + cb_ref[...]
        gates = _dot(xc.astype(BF16), wg_ref[:, 2 * d * W:(2 * d + 2) * W]) \
            + bg_ref[:, 2 * d * W:(2 * d + 2) * W]
        r = _sigmoid(gates[:, :W])
        i = _sigmoid(gates[:, W:])
        log_a = nsp[d:d + 1, :] * r
        a = jnp.exp(log_a)
        th = jnp.tanh(log_a)
        mult = jnp.sqrt(jnp.maximum(-2.0 * th / (1.0 - th), 0.0))
        t = r0 + lax.broadcasted_iota(jnp.int32, (R, W), 0)
        first = t == (S - 1 if d == 1 else 0)
        mult = jnp.where(first, 1.0, mult)
        acum, h = _scan_rows(a, mult * (i * xc), reverse=(d == 1))
        h = h + acum * carry_s[...]
        carry_s[...] = h[0:1, :] if d == 1 else h[R - 1:R, :]
        return r0, h

    carry_s[...] = jnp.zeros((1, W), F32)

    def fwd(n, c):
        r0, h = direction(0, n)
        hf_s[pl.ds(r0, R), :] = h
        return c

    lax.fori_loop(0, nchunk, fwd, 0)
    carry_s[...] = jnp.zeros((1, W), F32)

    def bwd(n, c):
        r0, h = direction(1, nchunk - 1 - n)
        gate = in_ref[pl.ds(r0, R), W:2 * W].astype(F32)
        o_ref[pl.ds(r0, R), :] = ((hf_s[pl.ds(r0, R), :] + h) * _gelu_tanh(gate)).astype(BF16)
        return c

    lax.fori_loop(0, nchunk, bwd, 0)


def _rglru(rg_in, cw, cb, wg, bg, lam, B, S):
    R = min(256, S)
    kern = functools.partial(_rglru_kernel, S=S, R=R)
    return pl.pallas_call(
        kern,
        grid=(B,),
        in_specs=[pl.BlockSpec((S, W_RG), lambda b: (b, 0)),
                  _const_spec(cw.shape), _const_spec(cb.shape), _const_spec(wg.shape),
                  _const_spec(bg.shape), _const_spec(lam.shape)],
        out_specs=pl.BlockSpec((S, RG_WIDTH), lambda b: (b, 0)),
        out_shape=jax.ShapeDtypeStruct((B * S, RG_WIDTH), BF16),
        scratch_shapes=[pltpu.VMEM((S + 16, RG_WIDTH), F32),
                        pltpu.VMEM((S, RG_WIDTH), F32),
                        pltpu.VMEM((1, RG_WIDTH), F32)],
        compiler_params=_cparams("arbitrary"),
        name="rglru",
    )(rg_in, cw, cb, wg, bg, lam)


def _tri_consts():
    C = GDN_CHUNK
    r = lax.broadcasted_iota(jnp.int32, (2 * C, 2 * C), 0)
    c = lax.broadcasted_iota(jnp.int32, (2 * C, 2 * C), 1)
    same = (r // C) == (c // C)
    one = lambda m: jnp.where(m, 1.0, 0.0).astype(BF16)
    return r, c, same, one


def _unit_tri_inverse(a_list, ri, ci, eye, width):
    blk = lambda n: (ri // n) == (ci // n)
    ident = jnp.where(eye, 1.0, 0.0)
    ps = [jnp.where(blk(8), -a, 0.0).astype(BF16) for a in a_list]
    ts = [ident + p.astype(F32) for p in ps]
    for _ in range(2):
        ps = [_dot(p, p).astype(BF16) for p in ps]
        ts = [t + _dot(t.astype(BF16), p) for t, p in zip(ts, ps)]
    n = 8
    while n < width:
        sel = blk(2 * n) & jnp.logical_not(blk(n))
        offs = [jnp.where(sel, a, 0.0).astype(BF16) for a in a_list]
        tbs = [t.astype(BF16) for t in ts]
        mids = [_dot(tb, off).astype(BF16) for tb, off in zip(tbs, offs)]
        ts = [t - _dot(mid, tb) for t, mid, tb in zip(ts, mids, tbs)]
        n *= 2
    return ts


def _gdn_kernel(in_ref, ab_ref, abt_ref, cw_ref, alr_ref, dtr_ref, alc_ref, dtc_ref, ng_ref,
                o_ref, xpad, qn_s, kn_s, vn_s, gcol_s, bcol_s, grow_s, rrow_s, brow_s,
                m1_s, m2_s, st_s, o_s, *, S):
    C = GDN_CHUNK
    H = GDN_HEADS
    NC = S // C
    HW = H * GDN_DK
    R = min(256, S)

    xpad[0:8, :] = jnp.zeros((8, HW), F32)
    xpad[S + 8:S + 16, :] = jnp.zeros((8, HW), F32)
    for grp, dst in enumerate((qn_s, kn_s, vn_s)):
        xpad[8:S + 8, :] = in_ref[:, grp * HW:(grp + 1) * HW].astype(F32)

        def conv_body(n, c, grp=grp, dst=dst):
            r0 = pl.multiple_of(n * R, R)
            xw = xpad[pl.ds(r0, R + 16), :]
            y = _silu(_conv4_window(xw, cw_ref.at[:, grp * HW:(grp + 1) * HW], R))
            if grp < 2:
                for h in range(H):
                    t = y[:, h * LANE:(h + 1) * LANE]
                    t = t * lax.rsqrt(jnp.sum(t * t, axis=-1, keepdims=True) + EPS)
                    dst[pl.ds(r0, R), h * LANE:(h + 1) * LANE] = t.astype(BF16)
            else:
                dst[pl.ds(r0, R), :] = y.astype(BF16)
            return c

        lax.fori_loop(0, S // R, conv_body, 0)

    r, c, same, one = _tri_consts()
    lane = lax.broadcasted_iota(jnp.int32, (2 * C, LANE), 1)
    is_fwd_lane = lane < H
    sub = lax.broadcasted_iota(jnp.int32, (N_STREAMS, LANE), 0)
    is_fwd_sub = sub < H
    tl_incl = one(same & (c <= r))
    tu_incl = one(same & (c >= r))
    lanec = lax.broadcasted_iota(jnp.int32, (2 * C, LANE), 1)

    def sum3(parts, mat, left):
        acc = None
        for p_ in parts:
            t = _dot(mat, p_) if left else _dot(p_, mat)
            acc = t if acc is None else acc + t
        return acc

    for pr in range(S // (2 * C)):
        rows = slice(pr * 2 * C, (pr + 1) * 2 * C)
        ab = ab_ref[rows, :]
        g_col = -jnp.exp(alr_ref[...]) * _softplus(ab + dtr_ref[...])
        parts = _split3(g_col)
        gcol_s[rows, :] = jnp.where(is_fwd_lane, sum3(parts, tl_incl, True), sum3(parts, tu_incl, True))
        bcol_s[rows, :] = _sigmoid(ab)
        abt = abt_ref[:, rows]
        g_row = -jnp.exp(alc_ref[...]) * _softplus(abt[0:N_STREAMS, :] + dtc_ref[...])
        b_row = _sigmoid(abt[N_STREAMS:2 * N_STREAMS, :])
        parts = _split3(g_row)
        for half in range(2):
            in_half = (r // C) == half
            cj = c % C
            rj = r % C
            incl_f = one(in_half & (rj <= cj))
            incl_b = one(in_half & (rj >= cj))
            excl_f = one(in_half & (rj > cj))
            excl_b = one(in_half & (rj < cj))
            ch = pr * 2 + half
            grow_s[ch] = jnp.where(is_fwd_sub, sum3(parts, incl_f, False), sum3(parts, incl_b, False))
            rrow_s[ch] = jnp.where(is_fwd_sub, sum3(parts, excl_f, False), sum3(parts, excl_b, False))
            dup = one(in_half & (rj == cj))
            brow_s[ch] = _dot(b_row.astype(BF16), dup) + _dot((b_row - b_row.astype(BF16).astype(F32)).astype(BF16), dup)

    P = 2 * C
    ri = lax.broadcasted_iota(jnp.int32, (P, P), 0)
    ci = lax.broadcasted_iota(jnp.int32, (P, P), 1)
    eye = ri == ci
    top = ri < C
    same_dir = (ri // C) == (ci // C)
    bot = jnp.logical_not(top)
    incl = same_dir & ((top & (ci <= ri)) | (bot & (ci >= ri)))
    strict = same_dir & ((top & (ci < ri)) | (bot & (ci > ri)))
    lane_f = lax.broadcasted_iota(jnp.int32, (H, LANE), 1) < C
    scale = GDN_DK ** -0.5

    def pair_rows(ref, rf, rb, lanes):
        return jnp.concatenate([ref[pl.ds(rf, C), lanes], ref[pl.ds(rb, C), lanes]], axis=0)

    def pair_lanes(arr_s, n, nb):
        return jnp.where(lane_f, arr_s[n][0:H], arr_s[nb][H:2 * H])

    def col_pair(col_f, col_b, lane_idx_f, lane_idx_b):
        return jnp.concatenate([jnp.broadcast_to(col_f[:, lane_idx_f:lane_idx_f + 1], (C, P)),
                                jnp.broadcast_to(col_b[:, lane_idx_b:lane_idx_b + 1], (C, P))], axis=0)

    def phase1(i, carry):
        a_l, decay_l, qk_l, brow_l = [], [], [], []
        for n in (2 * i, 2 * i + 1):
            nb = NC - 1 - n
            rf = pl.multiple_of(n * C, C)
            rb = pl.multiple_of(nb * C, C)
            gcol_f, gcol_b = gcol_s[pl.ds(rf, C), :], gcol_s[pl.ds(rb, C), :]
            bcol_f, bcol_b = bcol_s[pl.ds(rf, C), :], bcol_s[pl.ds(rb, C), :]
            grow = pair_lanes(grow_s, n, nb)
            brow = pair_lanes(brow_s, n, nb)
            for h in range(H):
                lanes = slice(h * LANE, (h + 1) * LANE)
                kp = pair_rows(kn_s, rf, rb, lanes)
                qp = pair_rows(qn_s, rf, rb, lanes)
                kq = _dot_nt(jnp.concatenate([kp, qp], axis=0), kp)
                gi = col_pair(gcol_f, gcol_b, h, H + h)
                bi = col_pair(bcol_f, bcol_b, N_STREAMS + h, N_STREAMS + H + h)
                decay = jnp.where(incl, jnp.exp(jnp.minimum(gi - grow[h:h + 1, :], 0.0)), 0.0)
                a_l.append(jnp.where(strict, bi * kq[0:P] * decay, 0.0))
                decay_l.append(decay)
                qk_l.append(kq[P:2 * P])
                brow_l.append(brow[h:h + 1, :])
        t_l = _unit_tri_inverse(a_l, ri, ci, eye, C)
        for j in range(2 * H):
            m1_s[2 * i * H + j] = (t_l[j] * brow_l[j]).astype(BF16)
            m2_s[2 * i * H + j] = (qk_l[j] * decay_l[j] * scale).astype(BF16)
        return carry

    lax.fori_loop(0, NC // 2, phase1, 0)

    st_s[...] = jnp.zeros(st_s.shape, F32)
    o_s[...] = jnp.zeros(o_s.shape, F32)
    zero_pp = jnp.zeros((P, P), BF16)

    def phase2(n, carry):
        nb = NC - 1 - n
        rf = pl.multiple_of(n * C, C)
        rb = pl.multiple_of(nb * C, C)
        grow = pair_lanes(grow_s, n, nb)
        rrow = pair_lanes(rrow_s, n, nb)
        eg = jnp.exp(grow)
        ek = jnp.exp(rrow)
        etot_f = jnp.exp(grow_s[n] + rrow_s[n])
        etot_b = jnp.exp(grow_s[nb] + rrow_s[nb])
        kp_l, kqs_l, vp_l = [], [], []
        for h in range(H):
            lanes = slice(h * LANE, (h + 1) * LANE)
            kp = pair_rows(kn_s, rf, rb, lanes)
            qp = pair_rows(qn_s, rf, rb, lanes)
            kp_l.append(kp)
            vp_l.append(pair_rows(vn_s, rf, rb, lanes))
            kqs_l.append(_dot(jnp.concatenate([kp, qp], axis=0), st_s[h].astype(BF16)))
        vnew_l, qs_l = [], []
        for h in range(H):
            kqs = kqs_l[h]
            ks = jnp.concatenate([kqs[0:C, 0:LANE], kqs[C:P, LANE:2 * LANE]], axis=0)
            qs_l.append(jnp.concatenate([kqs[P:P + C, 0:LANE], kqs[P + C:2 * P, LANE:2 * LANE]], axis=0))
            tb = m1_s[n * H + h].astype(F32)
            m1 = jnp.concatenate([tb, -(tb * eg[h:h + 1, :])], axis=1).astype(BF16)
            r1 = jnp.concatenate([vp_l[h], ks.astype(BF16)], axis=0)
            vnew_l.append(_dot(m1, r1))
        ox_l = []
        for h in range(H):
            dg = jnp.where(eye, eg[h:h + 1, :] * scale, 0.0).astype(BF16)
            dk = jnp.where(eye, ek[h:h + 1, :], 0.0).astype(BF16)
            m2 = jnp.concatenate([jnp.concatenate([m2_s[n * H + h], dg], axis=1),
                                  jnp.concatenate([dk, zero_pp], axis=1)], axis=0)
            r2 = jnp.concatenate([vnew_l[h].astype(BF16), qs_l[h].astype(BF16)], axis=0)
            ox_l.append(_dot(m2, r2))
        for h in range(H):
            lanes = slice(h * LANE, (h + 1) * LANE)
            x = ox_l[h][P:2 * P]
            xbd = jnp.concatenate([jnp.where(top, x, 0.0), jnp.where(top, 0.0, x)], axis=1).astype(BF16)
            etot = jnp.concatenate([etot_f[h:h + 1, :], etot_b[H + h:H + h + 1, :]], axis=1)
            st_s[h] = st_s[h] * etot + _dot_tn(kp_l[h], xbd)
            o_s[pl.ds(rf, C), lanes] += ox_l[h][0:C]
            o_s[pl.ds(rb, C), lanes] += ox_l[h][C:P]
        return carry

    lax.fori_loop(0, NC, phase2, 0)

    def phase3(n, carry):
        r0 = pl.multiple_of(n * R, R)
        z = in_ref[pl.ds(r0, R), 3 * HW:4 * HW].astype(F32)
        for h in range(H):
            lanes = slice(h * LANE, (h + 1) * LANE)
            o = o_s[pl.ds(r0, R), lanes]
            o = _rms(o, ng_ref[...]) * _silu(z[:, lanes])
            o_ref[pl.ds(r0, R), lanes] = o.astype(BF16)
        return carry

    lax.fori_loop(0, S // R, phase3, 0)


def _gdn(gdn_in, ab, abt, cw, alr, dtr, alc, dtc, ng, B, S):
    C = GDN_CHUNK
    NC = S // C
    HW = GDN_HEADS * GDN_DK
    kern = functools.partial(_gdn_kernel, S=S)
    scratch = [pltpu.VMEM((S + 16, HW), F32),
               pltpu.VMEM((S, HW), BF16), pltpu.VMEM((S, HW), BF16), pltpu.VMEM((S, HW), BF16),
               pltpu.VMEM((S, LANE), F32), pltpu.VMEM((S, LANE), F32),
               pltpu.VMEM((NC, N_STREAMS, LANE), F32), pltpu.VMEM((NC, N_STREAMS, LANE), F32),
               pltpu.VMEM((NC, N_STREAMS, LANE), F32),
               pltpu.VMEM((NC * GDN_HEADS, 2 * C, 2 * C), BF16),
               pltpu.VMEM((NC * GDN_HEADS, 2 * C, 2 * C), BF16),
               pltpu.VMEM((GDN_HEADS, GDN_DK, 2 * GDN_DV), F32),
               pltpu.VMEM((S, HW), F32)]
    return pl.pallas_call(
        kern,
        grid=(B,),
        in_specs=[pl.BlockSpec((S, W_GDN), lambda b: (b, 0), pipeline_mode=pl.Buffered(1)),
                  pl.BlockSpec((S, W_AB), lambda b: (b, 0)),
                  pl.BlockSpec((16, S), lambda b: (0, b)),
                  _const_spec(cw.shape), _const_spec(alr.shape), _const_spec(dtr.shape),
                  _const_spec(alc.shape), _const_spec(dtc.shape), _const_spec(ng.shape)],
        out_specs=pl.BlockSpec((S, HW), lambda b: (b, 0)),
        out_shape=jax.ShapeDtypeStruct((B * S, HW), BF16),
        scratch_shapes=scratch,
        compiler_params=_cparams("arbitrary"),
        name="gated_deltanet",
    )(gdn_in, ab, abt, cw, alr, dtr, alc, dtc, ng)


def _merge_kernel(x_ref, mod_ref, om_ref, or_ref, og_ref, gates_ref, wb_ref, wo_ref, o_ref):
    D = D_MODEL
    mixed = None
    for n, src in enumerate((om_ref, or_ref, og_ref)):
        g = _sigmoid(gates_ref[:, n * D:(n + 1) * D].astype(F32))
        term = g * _dot(src[...], wb_ref[n])
        mixed = term if mixed is None else mixed + term
    gt1 = mod_ref[:, 2 * D:3 * D]
    o_ref[...] = x_ref[...] + gt1 * _dot(mixed.astype(BF16), wo_ref[...])


def _merge(x2d, mod, o_mla, o_rg, o_gdn, gates, wb, wo, S, tm):
    T, D = x2d.shape
    per_seq = S // tm
    row = lambda i: (i, 0)
    return pl.pallas_call(
        _merge_kernel,
        grid=(T // tm,),
        in_specs=[pl.BlockSpec((tm, D), row),
                  pl.BlockSpec((None, 1, 6 * D), lambda i: (i // per_seq, 0, 0)),
                  pl.BlockSpec((tm, 512), row), pl.BlockSpec((tm, 512), row), pl.BlockSpec((tm, 512), row),
                  pl.BlockSpec((tm, W_GATES), row),
                  _const_spec(wb.shape), _const_spec(wo.shape)],
        out_specs=pl.BlockSpec((tm, D), row),
        out_shape=jax.ShapeDtypeStruct((T, D), F32),
        compiler_params=_cparams("arbitrary"),
        name="branch_merge",
    )(x2d, mod, o_mla, o_rg, o_gdn, gates, wb, wo)


def _ffn_kernel(x_ref, xp_ref, xn_ref, mod_ref, g_ref, wu_ref, cw_ref, cb_ref, wd_ref, fg_ref,
                o_ref, *, tm, per_seq, final):
    D = D_MODEL
    i = pl.program_id(0)
    sh = mod_ref[:, 3 * D:4 * D]
    sc = mod_ref[:, 4 * D:5 * D]
    gt2 = mod_ref[:, 5 * D:6 * D]
    x = x_ref[...]
    h = (_rms(x, g_ref[...]) * (1.0 + sc) + sh).astype(BF16)
    xh = jnp.concatenate([xp_ref[...], xn_ref[...]], axis=0)
    hh = (_rms(xh, g_ref[...]) * (1.0 + sc) + sh).astype(BF16)
    has_prev = jnp.where(i % per_seq == 0, 0.0, 1.0)
    has_next = jnp.where(i % per_seq == per_seq - 1, 0.0, 1.0)
    row = lax.broadcasted_iota(jnp.int32, (tm, 2 * FFN_CK), 0)
    acc = None
    for j in range(D_FF // FFN_CK):
        w = wu_ref[j]
        up = _dot(h, w)
        uh = _dot(hh, w)
        prev = uh[7:8, :] * has_prev
        nxt = uh[8:9, :] * has_next
        dn = jnp.where(row == 0, prev, pltpu.roll(up, 1, 0))
        un = jnp.where(row == tm - 1, nxt, pltpu.roll(up, tm - 1, 0))
        cw = cw_ref[j]
        y = dn * cw[0:1, :] + up * cw[1:2, :] + un * cw[2:3, :] + cb_ref[j]
        act = _silu(y[:, FFN_CK:]) * y[:, :FFN_CK]
        part = _dot(act.astype(BF16), wd_ref[j])
        acc = part if acc is None else acc + part
    y = x + gt2 * acc
    if final:
        y = _rms(y, fg_ref[...])
    o_ref[...] = y


def _ffn(x2d, mod, ln_g, wu, cw, cb, wd, fg, S, tm, final):
    T, D = x2d.shape
    per_seq = S // tm
    hb = tm // 8
    nblk8 = T // 8
    kern = functools.partial(_ffn_kernel, tm=tm, per_seq=per_seq, final=final)
    return pl.pallas_call(
        kern,
        grid=(T // tm,),
        in_specs=[pl.BlockSpec((tm, D), lambda i: (i, 0)),
                  pl.BlockSpec((8, D), lambda i: (jnp.maximum(i * hb - 1, 0), 0)),
                  pl.BlockSpec((8, D), lambda i: (jnp.minimum((i + 1) * hb, nblk8 - 1), 0)),
                  pl.BlockSpec((None, 1, 6 * D), lambda i: (i // per_seq, 0, 0)),
                  _const_spec((1, D)),
                  _const_spec(wu.shape), _const_spec(cw.shape), _const_spec(cb.shape),
                  _const_spec(wd.shape), _const_spec((1, D))],
        out_specs=pl.BlockSpec((tm, D), lambda i: (i, 0)),
        out_shape=jax.ShapeDtypeStruct((T, D), F32),
        compiler_params=_cparams("arbitrary"),
        name="conv_glu_ffn",
    )(x2d, x2d, x2d, mod, ln_g, wu, cw, cb, wd, fg)


def _rot_cols(w):
    half = w.shape[-1] // 2
    return jnp.concatenate([-w[..., half:], w[..., :half]], axis=-1)


def _prep_layer(l, p):
    D = D_MODEL
    w = p['w_in'][l]
    o = np.cumsum([0, MLA_Q_LORA, MLA_KV_LORA + MLA_ROPE, RG_WIDTH, RG_WIDTH, 3 * 512, 512, 8, 8, 3 * D])
    kro = o[1] + MLA_KV_LORA
    w_kr = w[:, kro:o[2]]
    w1 = jnp.concatenate([
        w[:, o[0]:o[1]], w[:, o[1]:kro], w_kr, _rot_cols(w_kr), jnp.zeros((D, 64), F32),
        w[:, o[2]:o[4]],
        w[:, o[4]:o[6]],
        w[:, o[6]:o[8]], jnp.zeros((D, W_AB - 16), F32),
        w[:, o[8]:o[9]]], axis=1).astype(BF16)
    wabt = w[:, o[6]:o[8]].T.astype(BF16)

    H = MLA_HEADS
    wq = p['mla_w_uq'][l].reshape(MLA_Q_LORA, H, MLA_QK)
    rope = wq[..., MLA_NOPE:]
    wq_ext = jnp.concatenate([wq[..., :MLA_NOPE], rope, _rot_cols(rope)], axis=-1)
    wq_ext = wq_ext.reshape(MLA_Q_LORA, H * HEAD_PAD).astype(BF16)
    wkv = p['mla_w_ukv'][l].reshape(MLA_KV_LORA, H, MLA_NOPE + MLA_V)
    wk_top = jnp.concatenate([wkv[..., :MLA_NOPE], jnp.zeros((MLA_KV_LORA, H, 64), F32)], axis=-1)
    e = np.zeros((128, H, HEAD_PAD), np.float32)
    for j in range(MLA_ROPE):
        for src in (j, MLA_ROPE + j):
            e[src, :, MLA_NOPE + j] = 1.0
            e[src, :, MLA_NOPE + MLA_ROPE + j] = 1.0
    wk_ext = jnp.concatenate([wk_top, jnp.asarray(e)], axis=0).reshape(MLA_KV_LORA + 128, H * HEAD_PAD).astype(BF16)
    wv = wkv[..., MLA_NOPE:]
    z = jnp.zeros_like(wv)
    even = (np.arange(H) % 2 == 0)[None, :, None]
    wv_ext = jnp.concatenate([jnp.where(even, wv, z), jnp.where(even, z, wv)], axis=-1)
    wv_ext = wv_ext.reshape(MLA_KV_LORA, H * HEAD_PAD).astype(BF16)

    eye = jnp.eye(RG_BLOCKS, dtype=F32)
    bd = lambda t: jnp.einsum('ncd,nm->ncmd', t, eye).reshape(RG_WIDTH, RG_WIDTH)
    wg = jnp.concatenate([bd(p['rg_w_a'][l, 0]), bd(p['rg_w_i'][l, 0]),
                          bd(p['rg_w_a'][l, 1]), bd(p['rg_w_i'][l, 1])], axis=1).astype(BF16)
    bg = jnp.concatenate([p['rg_b_a'][l, 0], p['rg_b_i'][l, 0], p['rg_b_a'][l, 1], p['rg_b_i'][l, 1]])[None, :]

    pad_row = lambda v: jnp.concatenate([v.reshape(-1), jnp.zeros((LANE - N_STREAMS,), F32)])[None, :]
    bc_col = lambda v: jnp.broadcast_to(v.reshape(-1, 1), (N_STREAMS, LANE))

    nck = D_FF // FFN_CK
    pair = lambda t: jnp.concatenate([t[..., :D_FF].reshape(t.shape[:-1] + (nck, FFN_CK)),
                                      t[..., D_FF:].reshape(t.shape[:-1] + (nck, FFN_CK))], axis=-1)
    wu = jnp.moveaxis(pair(p['ffn_w_up'][l]), 1, 0).astype(BF16)
    fcw = jnp.moveaxis(pair(p['ffn_conv_w'][l]), 1, 0)
    fcb = pair(p['ffn_conv_b'][l])[:, None, :]
    wd = p['ffn_w_down'][l].reshape(nck, FFN_CK, D).astype(BF16)

    return dict(
        ln1_g=p['ln1_g'][l][None, :], w1=w1, wabt=wabt,
        gq=p['mla_q_norm_g'][l][None, :], wq=wq_ext, gkv=p['mla_kv_norm_g'][l][None, :],
        wk=wk_ext, wv=wv_ext,
        rg_cw=p['rg_conv_w'][l], rg_cb=p['rg_conv_b'][l][None, :], wg=wg, bg=bg, lam=p['rg_lam'][l],
        gdn_cw=p['gdn_conv_w'][l], alr=pad_row(p['gdn_a_log'][l]), dtr=pad_row(p['gdn_dt_bias'][l]),
        alc=bc_col(p['gdn_a_log'][l]), dtc=bc_col(p['gdn_dt_bias'][l]), ng=p['gdn_norm_g'][l][None, :],
        wb=p['w_branch'][l].astype(BF16), wo=p['w_out'][l].astype(BF16),
        ln2_g=p['ln2_g'][l][None, :], wu=wu, fcw=fcw, fcb=fcb, wd=wd)


def _rope_tables(S):
    inv = 1.0 / (ROPE_THETA ** (jnp.arange(0, MLA_ROPE, 2, dtype=F32) / MLA_ROPE))
    ang = jnp.arange(S, dtype=F32)[:, None] * inv[None, :]
    cos, sin = jnp.cos(ang), jnp.sin(ang)
    scale = MLA_QK ** -0.5
    csq = scale * jnp.concatenate([jnp.ones((S, MLA_NOPE), F32), cos, cos, sin, sin], axis=1)
    csk = jnp.concatenate([cos, cos, sin, sin, jnp.zeros((S, 64), F32)], axis=1)
    return csq, csk


def _trunk(x, mod, layers, final_g, csq, csk):
    B, S, D = x.shape
    x2d = x.reshape(B * S, D)
    tm = min(256, S)
    for l, w in enumerate(layers):
        m = mod[l].reshape(B, 1, 6 * D)
        mla_in, rg_in, gdn_in, ab, abt, gates = _in_projection(x2d, m, w['ln1_g'], w['w1'], w['wabt'], S, tm)
        o_mla = _mla(mla_in, csq, csk, w['gq'], w['wq'], w['gkv'], w['wk'], w['wv'], B, S)
        o_rg = _rglru(rg_in, w['rg_cw'], w['rg_cb'], w['wg'], w['bg'], w['lam'], B, S)
        o_gdn = _gdn(gdn_in, ab, abt, w['gdn_cw'], w['alr'], w['dtr'], w['alc'], w['dtc'], w['ng'], B, S)
        x2d = _merge(x2d, m, o_mla, o_rg, o_gdn, gates, w['wb'], w['wo'], S, tm)
        x2d = _ffn(x2d, m, w['ln2_g'], w['wu'], w['fcw'], w['fcb'], w['wd'], final_g, S, min(512, S),
                   final=(l == len(layers) - 1))
    return x2d.reshape(B, S, D)


def kernel(x_prompt, x_sample, c_prompt, c_sample, ln1_g, w_mod, b_mod, w_in, mla_q_norm_g, mla_w_uq, mla_kv_norm_g, mla_w_ukv, rg_conv_w, rg_conv_b, rg_w_a, rg_b_a, rg_w_i, rg_b_i, rg_lam, gdn_conv_w, gdn_a_log, gdn_dt_bias, gdn_norm_g, w_branch, w_out, ln2_g, ffn_w_up, ffn_conv_w, ffn_conv_b, ffn_w_down, final_norm_g):
    p = dict(ln1_g=ln1_g, w_in=w_in, mla_q_norm_g=mla_q_norm_g, mla_w_uq=mla_w_uq,
             mla_kv_norm_g=mla_kv_norm_g, mla_w_ukv=mla_w_ukv, rg_conv_w=rg_conv_w, rg_conv_b=rg_conv_b,
             rg_w_a=rg_w_a, rg_b_a=rg_b_a, rg_w_i=rg_w_i, rg_b_i=rg_b_i, rg_lam=rg_lam,
             gdn_conv_w=gdn_conv_w, gdn_a_log=gdn_a_log, gdn_dt_bias=gdn_dt_bias, gdn_norm_g=gdn_norm_g,
             w_branch=w_branch, w_out=w_out, ln2_g=ln2_g, ffn_w_up=ffn_w_up, ffn_conv_w=ffn_conv_w,
             ffn_conv_b=ffn_conv_b, ffn_w_down=ffn_w_down)
    L = w_in.shape[0]
    layers = [_prep_layer(l, p) for l in range(L)]
    Bp = x_prompt.shape[0]
    mod = _modulation(jnp.concatenate([c_prompt, c_sample], axis=0), w_mod, b_mod)
    fg = final_norm_g[None, :]
    outs = []
    for x, m in ((x_prompt, mod[:, :Bp]), (x_sample, mod[:, Bp:])):
        csq, csk = _rope_tables(x.shape[1])
        outs.append(_trunk(x, m, layers, fg, csq, csk))
    return tuple(outs)
```

```python
import functools
import math

import jax
import jax.numpy as jnp
import numpy as np
from jax import lax
from jax.experimental import pallas as pl
from jax.experimental.pallas import tpu as pltpu

D_MODEL = 1024
DEPTH = 2
MLA_HEADS = 8
MLA_Q_LORA = 384
MLA_KV_LORA = 256
MLA_NOPE = 64
MLA_ROPE = 32
MLA_V = 64
MLA_QK = MLA_NOPE + MLA_ROPE
ROPE_THETA = 10000.0
RG_WIDTH = 512
RG_BLOCKS = 8
RG_C = 8.0
GDN_HEADS = 4
GDN_DK = 128
GDN_DV = 128
GDN_CHUNK = 64
D_FF = 2816
EPS = 1e-6

LANE = 128
HEAD_PAD = 128
VMEM_LIMIT = 56 * 1024 * 1024
FFN_CK = 256
N_STREAMS = 2 * GDN_HEADS
GDN_PAIRS_PER_STEP = 2

F32 = jnp.float32
BF16 = jnp.bfloat16


def _cparams(*sem):
    return pltpu.CompilerParams(dimension_semantics=sem, vmem_limit_bytes=VMEM_LIMIT)


def _const_spec(shape):
    nd = len(shape)
    return pl.BlockSpec(shape, lambda *_: (0,) * nd, pipeline_mode=pl.Buffered(1))


def _sigmoid(x):
    return 0.5 * jnp.tanh(0.5 * x) + 0.5


def _silu(x):
    return x * _sigmoid(x)


def _softplus(x):
    return jnp.maximum(x, 0.0) + jnp.log(1.0 + jnp.exp(-jnp.abs(x)))


def _gelu_tanh(x):
    c = math.sqrt(2.0 / math.pi)
    return 0.5 * x * (1.0 + jnp.tanh(c * (x + 0.044715 * (x * x * x))))


def _rms(x, g):
    return x * lax.rsqrt(jnp.mean(x * x, axis=-1, keepdims=True) + EPS) * g


def _dot(a, b):
    return jnp.dot(a, b, preferred_element_type=F32)


def _dot_nt(a, b):
    return lax.dot_general(a, b, (((1,), (1,)), ((), ())), preferred_element_type=F32)


def _dot_tn(a, b):
    return lax.dot_general(a, b, (((0,), (0,)), ((), ())), preferred_element_type=F32)


def _split3(x):
    x1 = x.astype(BF16)
    r1 = x - x1.astype(F32)
    x2 = r1.astype(BF16)
    x3 = (r1 - x2.astype(F32)).astype(BF16)
    return x1, x2, x3


def _mod_kernel(c_ref, w_ref, b_ref, o_ref):
    c = c_ref[...]
    a1, a2, a3 = _split3(_silu(c))
    w1, w2, w3 = _split3(w_ref[...])
    acc = _dot(a1, w1) + (_dot(a1, w2) + _dot(a2, w1)) + (_dot(a1, w3) + _dot(a2, w2) + _dot(a3, w1))
    o_ref[...] = acc + b_ref[...]


def _modulation(c, w_mod, b_mod):
    L, D, N = w_mod.shape
    Bt = c.shape[0]
    tn = 1024
    return pl.pallas_call(
        _mod_kernel,
        grid=(L, N // tn),
        in_specs=[pl.BlockSpec((Bt, D), lambda l, j: (0, 0)),
                  pl.BlockSpec((None, D, tn), lambda l, j: (l, 0, j)),
                  pl.BlockSpec((None, 1, tn), lambda l, j: (l, 0, j))],
        out_specs=pl.BlockSpec((None, Bt, tn), lambda l, j: (l, 0, j)),
        out_shape=jax.ShapeDtypeStruct((L, Bt, N), F32),
        compiler_params=_cparams("arbitrary", "arbitrary"),
        name="modulation",
    )(c, w_mod, b_mod.reshape(L, 1, N))


W_MLA, W_RG, W_GDN, W_AB, W_GATES = 768, 1024, 2048, 128, 3 * D_MODEL
_IN_OFFS = np.cumsum([0, W_MLA, W_RG, W_GDN, W_AB, W_GATES])


def _inproj_kernel(x_ref, mod_ref, g_ref, w_ref, wabt_ref,
                   mla_ref, rg_ref, gdn_ref, ab_ref, abt_ref, gates_ref):
    D = D_MODEL
    x = x_ref[...]
    sh = mod_ref[:, 0:D]
    sc = mod_ref[:, D:2 * D]
    h = (_rms(x, g_ref[...]) * (1.0 + sc) + sh).astype(BF16)
    o = _IN_OFFS
    mla_ref[...] = _dot(h, w_ref[:, o[0]:o[1]]).astype(BF16)
    rg_ref[...] = _dot(h, w_ref[:, o[1]:o[2]]).astype(BF16)
    gdn_ref[...] = _dot(h, w_ref[:, o[2]:o[3]]).astype(BF16)
    ab_ref[...] = _dot(h, w_ref[:, o[3]:o[4]])
    gates_ref[...] = _dot(h, w_ref[:, o[4]:o[5]]).astype(BF16)
    abt_ref[...] = _dot_nt(wabt_ref[...], h)


def _in_projection(x2d, mod, ln_g, w1, wabt, S, tm):
    T, D = x2d.shape
    per_seq = S // tm
    row = lambda i: (i, 0)
    outs = [(W_MLA, BF16), (W_RG, BF16), (W_GDN, BF16), (W_AB, F32)]
    out_shape = [jax.ShapeDtypeStruct((T, w), dt) for w, dt in outs]
    out_specs = [pl.BlockSpec((tm, w), row) for w, _ in outs]
    out_shape += [jax.ShapeDtypeStruct((16, T), F32), jax.ShapeDtypeStruct((T, W_GATES), BF16)]
    out_specs += [pl.BlockSpec((16, tm), lambda i: (0, i)), pl.BlockSpec((tm, W_GATES), row)]
    return pl.pallas_call(
        _inproj_kernel,
        grid=(T // tm,),
        in_specs=[pl.BlockSpec((tm, D), row),
                  pl.BlockSpec((None, 1, 6 * D), lambda i: (i // per_seq, 0, 0)),
                  _const_spec((1, D)),
                  _const_spec(w1.shape),
                  _const_spec(wabt.shape)],
        out_specs=out_specs,
        out_shape=out_shape,
        compiler_params=_cparams("arbitrary"),
        name="in_projection",
    )(x2d, mod, ln_g, w1, wabt)


def _mla_kernel(in_ref, csq_ref, csk_ref, gq_ref, wq_ref, gkv_ref, wk_ref, wv_ref,
                o_ref, q_s, k_s, v_s, *, S, tq, tr):
    hp = pl.program_id(1)

    @pl.when(hp == 0)
    def _project():
        for r in range(S // tr):
            rows = slice(r * tr, (r + 1) * tr)
            qd = in_ref[rows, 0:MLA_Q_LORA].astype(F32)
            ckv = in_ref[rows, MLA_Q_LORA:MLA_Q_LORA + MLA_KV_LORA].astype(F32)
            kr = in_ref[rows, 640:768].astype(F32) * csk_ref[rows, :]
            cq = _rms(qd, gq_ref[...]).astype(BF16)
            ckvn = _rms(ckv, gkv_ref[...]).astype(BF16)
            q = _dot(cq, wq_ref[...])
            kin = jnp.concatenate([ckvn, kr.astype(BF16)], axis=1)
            k = _dot(kin, wk_ref[...])
            v = _dot(ckvn, wv_ref[...])
            cs = csq_ref[rows, :]
            for p in range(MLA_HEADS // 2):
                for hh in range(2):
                    h = 2 * p + hh
                    q_s[p, rows, hh * LANE:(hh + 1) * LANE] = (
                        q[:, h * LANE:(h + 1) * LANE] * cs).astype(BF16)
                k_s[p, rows, :] = k[:, 2 * p * LANE:(2 * p + 2) * LANE].astype(BF16)
                v_s[p, rows, :] = v[:, 2 * p * LANE:(2 * p + 2) * LANE].astype(BF16)

    for qt in range(S // tq):
        rows = slice(qt * tq, (qt + 1) * tq)
        acc = None
        for hh in range(2):
            lanes = slice(hh * LANE, (hh + 1) * LANE)
            q = q_s[hp, rows, lanes]
            k = k_s[hp, :, lanes]
            s = _dot_nt(q, k)
            m = jnp.max(s, axis=-1, keepdims=True)
            p = jnp.exp(s - m)
            l = jnp.sum(p, axis=-1, keepdims=True)
            o = _dot(p.astype(BF16), v_s[hp, :, lanes]) * (1.0 / l)
            acc = o if acc is None else acc + o
        o_ref[rows, :] = acc.astype(BF16)


def _mla(mla_in, csq, csk, gq, wq, gkv, wk, wv, B, S):
    tq = min(512, S)
    tr = min(512, S)
    npair = MLA_HEADS // 2
    kern = functools.partial(_mla_kernel, S=S, tq=tq, tr=tr)
    return pl.pallas_call(
        kern,
        grid=(B, npair),
        in_specs=[pl.BlockSpec((S, W_MLA), lambda b, p: (b, 0)),
                  _const_spec(csq.shape), _const_spec(csk.shape),
                  _const_spec(gq.shape), _const_spec(wq.shape),
                  _const_spec(gkv.shape), _const_spec(wk.shape), _const_spec(wv.shape)],
        out_specs=pl.BlockSpec((S, LANE), lambda b, p: (b, p)),
        out_shape=jax.ShapeDtypeStruct((B * S, MLA_HEADS * MLA_V), BF16),
        scratch_shapes=[pltpu.VMEM((npair, S, 2 * LANE), BF16),
                        pltpu.VMEM((npair, S, 2 * LANE), BF16),
                        pltpu.VMEM((npair, S, 2 * LANE), BF16)],
        compiler_params=_cparams("arbitrary", "arbitrary"),
        name="mla_attention",
    )(mla_in, csq, csk, gq, wq, gkv, wk, wv)


def _conv4_window(xw, w_ref, R):
    acc = xw[7:7 + R] * w_ref[0:1, :]
    acc = acc + xw[8:8 + R] * w_ref[1:2, :]
    acc = acc + xw[9:9 + R] * w_ref[2:3, :]
    acc = acc + xw[10:10 + R] * w_ref[3:4, :]
    return acc


def _scan_rows(a, b, carry, reverse):
    R, W = a.shape
    a = a.reshape(R // 8, 8, W)
    b = b.reshape(R // 8, 8, W)
    sub = lax.broadcasted_iota(jnp.int32, a.shape, 1)
    for d in (1, 2, 4):
        shift, ok = (8 - d, sub < 8 - d) if reverse else (d, sub >= d)
        a_sh = pltpu.roll(a, shift, 1)
        b_sh = pltpu.roll(b, shift, 1)
        b = b + a * jnp.where(ok, b_sh, 0.0)
        a = a * jnp.where(ok, a_sh, 1.0)
    tiles = [None] * (R // 8)
    for j in (range(R // 8 - 1, -1, -1) if reverse else range(R // 8)):
        t = b[j] + a[j] * carry
        tiles[j] = t
        carry = t[0:1] if reverse else t[7:8]
    return jnp.concatenate(tiles, axis=0), carry


def _rglru_kernel(in_ref, cw_ref, cb_ref, wg_ref, bg_ref, lam_ref, o_ref,
                  xpad, hf_s, carry_s, *, S, R):
    W = RG_WIDTH
    nchunk = S // R
    xpad[0:8, :] = jnp.zeros((8, W), F32)
    xpad[S + 8:S + 16, :] = jnp.zeros((8, W), F32)
    xpad[8:S + 8, :] = in_ref[:, 0:W].astype(F32)
    nsp = -RG_C * _softplus(-lam_ref[...])

    def direction(d, n):
        r0 = pl.multiple_of(n * R, R)
        xw = xpad[pl.ds(r0, R + 16), :]
        xc = _conv4_window(xw, cw_ref, R) + cb_ref[...]
        gates = _dot(xc.astype(BF16), wg_ref[:, 2 * d * W:(2 * d + 2) * W]) \
            + bg_ref[:, 2 * d * W:(2 * d + 2) * W]
        r = _sigmoid(gates[:, :W])
        i = _sigmoid(gates[:, W:])
        log_a = nsp[d:d + 1, :] * r
        a = jnp.exp(log_a)
        th = jnp.tanh(log_a)
        mult = jnp.sqrt(jnp.maximum(-2.0 * th / (1.0 - th), 0.0))
        t = r0 + lax.broadcasted_iota(jnp.int32, (R, W), 0)
        first = t == (S - 1 if d == 1 else 0)
        mult = jnp.where(first, 1.0, mult)
        h, carry = _scan_rows(a, mult * (i * xc), carry_s[...], reverse=(d == 1))
        carry_s[...] = carry
        return r0, h

    carry_s[...] = jnp.zeros((1, W), F32)

    def fwd(n, c):
        r0, h = direction(0, n)
        hf_s[pl.ds(r0, R), :] = h
        return c

    lax.fori_loop(0, nchunk, fwd, 0)
    carry_s[...] = jnp.zeros((1, W), F32)

    def bwd(n, c):
        r0, h = direction(1, nchunk - 1 - n)
        gate = in_ref[pl.ds(r0, R), W:2 * W].astype(F32)
        o_ref[pl.ds(r0, R), :] = ((hf_s[pl.ds(r0, R), :] + h) * _gelu_tanh(gate)).astype(BF16)
        return c

    lax.fori_loop(0, nchunk, bwd, 0)


def _rglru(rg_in, cw, cb, wg, bg, lam, B, S):
    R = min(256, S)
    kern = functools.partial(_rglru_kernel, S=S, R=R)
    return pl.pallas_call(
        kern,
        grid=(B,),
        in_specs=[pl.BlockSpec((S, W_RG), lambda b: (b, 0)),
                  _const_spec(cw.shape), _const_spec(cb.shape), _const_spec(wg.shape),
                  _const_spec(bg.shape), _const_spec(lam.shape)],
        out_specs=pl.BlockSpec((S, RG_WIDTH), lambda b: (b, 0)),
        out_shape=jax.ShapeDtypeStruct((B * S, RG_WIDTH), BF16),
        scratch_shapes=[pltpu.VMEM((S + 16, RG_WIDTH), F32),
                        pltpu.VMEM((S, RG_WIDTH), F32),
                        pltpu.VMEM((1, RG_WIDTH), F32)],
        compiler_params=_cparams("arbitrary"),
        name="rglru",
    )(rg_in, cw, cb, wg, bg, lam)


def _tri_consts():
    C = GDN_CHUNK
    r = lax.broadcasted_iota(jnp.int32, (2 * C, 2 * C), 0)
    c = lax.broadcasted_iota(jnp.int32, (2 * C, 2 * C), 1)
    same = (r // C) == (c // C)
    one = lambda m: jnp.where(m, 1.0, 0.0).astype(BF16)
    return r, c, same, one


def _unit_tri_inverse(a_list, ri, ci, eye, width):
    blk = lambda n: (ri // n) == (ci // n)
    ident = jnp.where(eye, 1.0, 0.0)
    ps = [jnp.where(blk(8), -a, 0.0).astype(BF16) for a in a_list]
    ts = [ident + p.astype(F32) for p in ps]
    for _ in range(2):
        ps = [_dot(p, p).astype(BF16) for p in ps]
        ts = [t + _dot(t.astype(BF16), p) for t, p in zip(ts, ps)]
    n = 8
    while n < width:
        sel = blk(2 * n) & jnp.logical_not(blk(n))
        offs = [jnp.where(sel, a, 0.0).astype(BF16) for a in a_list]
        tbs = [t.astype(BF16) for t in ts]
        mids = [_dot(tb, off).astype(BF16) for tb, off in zip(tbs, offs)]
        ts = [t - _dot(mid, tb) for t, mid, tb in zip(ts, mids, tbs)]
        n *= 2
    return ts


def _gdn_kernel(in_ref, ab_ref, abt_ref, cw_ref, alr_ref, dtr_ref, alc_ref, dtc_ref, ng_ref,
                o_ref, xpad, qn_s, kn_s, vn_s, gcol_s, rcol_s, bcol_s, grow_s, rrow_s, brow_s,
                m1_s, m2_s, st_s, o_s, *, S):
    C = GDN_CHUNK
    H = GDN_HEADS
    NC = S // C
    HW = H * GDN_DK
    R = min(256, S)

    xpad[0:8, :] = jnp.zeros((8, HW), F32)
    xpad[S + 8:S + 16, :] = jnp.zeros((8, HW), F32)
    for grp, dst in enumerate((qn_s, kn_s, vn_s)):
        xpad[8:S + 8, :] = in_ref[:, grp * HW:(grp + 1) * HW].astype(F32)

        def conv_body(n, c, grp=grp, dst=dst):
            r0 = pl.multiple_of(n * R, R)
            xw = xpad[pl.ds(r0, R + 16), :]
            y = _silu(_conv4_window(xw, cw_ref.at[:, grp * HW:(grp + 1) * HW], R))
            if grp < 2:
                for h in range(H):
                    t = y[:, h * LANE:(h + 1) * LANE]
                    t = t * lax.rsqrt(jnp.sum(t * t, axis=-1, keepdims=True) + EPS)
                    dst[pl.ds(r0, R), h * LANE:(h + 1) * LANE] = t.astype(BF16)
            else:
                dst[pl.ds(r0, R), :] = y.astype(BF16)
            return c

        lax.fori_loop(0, S // R, conv_body, 0)

    r, c, same, one = _tri_consts()
    lane = lax.broadcasted_iota(jnp.int32, (2 * C, LANE), 1)
    is_fwd_lane = lane < H
    sub = lax.broadcasted_iota(jnp.int32, (N_STREAMS, LANE), 0)
    is_fwd_sub = sub < H
    tl_incl = one(same & (c <= r))
    tu_incl = one(same & (c >= r))
    tl_excl = one(same & (c < r))
    tu_excl = one(same & (c > r))

    def sum3(parts, mat, left):
        acc = None
        for p_ in parts:
            t = _dot(mat, p_) if left else _dot(p_, mat)
            acc = t if acc is None else acc + t
        return acc

    for pr in range(S // (2 * C)):
        rows = slice(pr * 2 * C, (pr + 1) * 2 * C)
        ab = ab_ref[rows, :]
        g_col = -jnp.exp(alr_ref[...]) * _softplus(ab + dtr_ref[...])
        parts = _split3(g_col)
        gcol_s[rows, :] = jnp.where(is_fwd_lane, sum3(parts, tl_incl, True), sum3(parts, tu_incl, True))
        rcol_s[rows, :] = jnp.where(is_fwd_lane, sum3(parts, tu_excl, True), sum3(parts, tl_excl, True))
        bcol_s[rows, :] = _sigmoid(ab)
        abt = abt_ref[:, rows]
        g_row = -jnp.exp(alc_ref[...]) * _softplus(abt[0:N_STREAMS, :] + dtc_ref[...])
        b_row = _sigmoid(abt[N_STREAMS:2 * N_STREAMS, :])
        parts = _split3(g_row)
        for half in range(2):
            in_half = (r // C) == half
            cj = c % C
            rj = r % C
            incl_f = one(in_half & (rj <= cj))
            incl_b = one(in_half & (rj >= cj))
            excl_f = one(in_half & (rj > cj))
            excl_b = one(in_half & (rj < cj))
            ch = pr * 2 + half
            grow_s[ch] = jnp.where(is_fwd_sub, sum3(parts, incl_f, False), sum3(parts, incl_b, False))
            rrow_s[ch] = jnp.where(is_fwd_sub, sum3(parts, excl_f, False), sum3(parts, excl_b, False))
            dup = one(in_half & (rj == cj))
            brow_s[ch] = _dot(b_row.astype(BF16), dup) + _dot((b_row - b_row.astype(BF16).astype(F32)).astype(BF16), dup)

    P = 2 * C
    ri = lax.broadcasted_iota(jnp.int32, (P, P), 0)
    ci = lax.broadcasted_iota(jnp.int32, (P, P), 1)
    eye = ri == ci
    top = ri < C
    same_dir = (ri // C) == (ci // C)
    bot = jnp.logical_not(top)
    incl = same_dir & ((top & (ci <= ri)) | (bot & (ci >= ri)))
    strict = same_dir & ((top & (ci < ri)) | (bot & (ci > ri)))
    lane_f = lax.broadcasted_iota(jnp.int32, (H, LANE), 1) < C
    scale = GDN_DK ** -0.5

    def pair_rows(ref, rf, rb, lanes):
        return jnp.concatenate([ref[pl.ds(rf, C), lanes], ref[pl.ds(rb, C), lanes]], axis=0)

    def pair_lanes(arr_s, n, nb):
        return jnp.where(lane_f, arr_s[n][0:H], arr_s[nb][H:2 * H])

    def col_pair(col_f, col_b, lane_idx_f, lane_idx_b):
        return jnp.concatenate([jnp.broadcast_to(col_f[:, lane_idx_f:lane_idx_f + 1], (C, P)),
                                jnp.broadcast_to(col_b[:, lane_idx_b:lane_idx_b + 1], (C, P))], axis=0)

    G = GDN_PAIRS_PER_STEP

    def phase1(i):
        a_l, decay_l, qk_l, brow_l = [], [], [], []
        for n in [G * i + g for g in range(G)]:
            nb = NC - 1 - n
            rf = pl.multiple_of(n * C, C)
            rb = pl.multiple_of(nb * C, C)
            gcol_f, gcol_b = gcol_s[pl.ds(rf, C), :], gcol_s[pl.ds(rb, C), :]
            bcol_f, bcol_b = bcol_s[pl.ds(rf, C), :], bcol_s[pl.ds(rb, C), :]
            grow = pair_lanes(grow_s, n, nb)
            brow = pair_lanes(brow_s, n, nb)
            for h in range(H):
                lanes = slice(h * LANE, (h + 1) * LANE)
                kp = pair_rows(kn_s, rf, rb, lanes)
                qp = pair_rows(qn_s, rf, rb, lanes)
                kq = _dot_nt(jnp.concatenate([kp, qp], axis=0), kp)
                gi = col_pair(gcol_f, gcol_b, h, H + h)
                bi = col_pair(bcol_f, bcol_b, N_STREAMS + h, N_STREAMS + H + h)
                decay = jnp.where(incl, jnp.exp(jnp.minimum(gi - grow[h:h + 1, :], 0.0)), 0.0)
                a_l.append(jnp.where(strict, bi * kq[0:P] * decay, 0.0))
                decay_l.append(decay)
                qk_l.append(kq[P:2 * P])
                brow_l.append(brow[h:h + 1, :])
        t_l = _unit_tri_inverse(a_l, ri, ci, eye, C)
        for j in range(G * H):
            m1_s[G * i * H + j] = (t_l[j] * brow_l[j]).astype(BF16)
            m2_s[G * i * H + j] = (qk_l[j] * decay_l[j] * scale).astype(BF16)

    st_s[...] = jnp.zeros(st_s.shape, F32)
    o_s[...] = jnp.zeros(o_s.shape, F32)

    def phase2(n):
        nb = NC - 1 - n
        rf = pl.multiple_of(n * C, C)
        rb = pl.multiple_of(nb * C, C)
        eg_f, eg_b = jnp.exp(gcol_s[pl.ds(rf, C), :]), jnp.exp(gcol_s[pl.ds(rb, C), :])
        ek_f, ek_b = jnp.exp(rcol_s[pl.ds(rf, C), :]), jnp.exp(rcol_s[pl.ds(rb, C), :])
        etot_f = jnp.exp(grow_s[n] + rrow_s[n])
        etot_b = jnp.exp(grow_s[nb] + rrow_s[nb])
        kp_l, kqs_l, vp_l = [], [], []
        for h in range(H):
            lanes = slice(h * LANE, (h + 1) * LANE)
            kp = pair_rows(kn_s, rf, rb, lanes)
            qp = pair_rows(qn_s, rf, rb, lanes)
            kp_l.append(kp)
            vp_l.append(pair_rows(vn_s, rf, rb, lanes))
            kqs_l.append(_dot(jnp.concatenate([kp, qp], axis=0), st_s[h].astype(BF16)))
        vnew_l, oq_l = [], []
        for h in range(H):
            kqs = kqs_l[h]
            ks = jnp.concatenate([kqs[0:C, 0:LANE], kqs[C:P, LANE:2 * LANE]], axis=0)
            qs = jnp.concatenate([kqs[P:P + C, 0:LANE], kqs[P + C:2 * P, LANE:2 * LANE]], axis=0)
            egc = col_pair(eg_f, eg_b, h, H + h)
            oq_l.append(egc * scale * qs)
            r1 = (vp_l[h].astype(F32) - egc * ks).astype(BF16)
            vnew_l.append(_dot(m1_s[n * H + h], r1))
        for h in range(H):
            lanes = slice(h * LANE, (h + 1) * LANE)
            v_new = vnew_l[h]
            o = _dot(m2_s[n * H + h], v_new.astype(BF16)) + oq_l[h]
            x = col_pair(ek_f, ek_b, h, H + h) * v_new
            xbd = jnp.concatenate([jnp.where(top, x, 0.0), jnp.where(top, 0.0, x)], axis=1).astype(BF16)
            etot = jnp.concatenate([etot_f[h:h + 1, :], etot_b[H + h:H + h + 1, :]], axis=1)
            st_s[h] = st_s[h] * etot + _dot_tn(kp_l[h], xbd)
            o_s[pl.ds(rf, C), lanes] += o[0:C]
            o_s[pl.ds(rb, C), lanes] += o[C:P]

    nstep = NC // G
    phase1(0)

    def fused(i, carry):
        for g in range(G):
            phase2(G * i + g)
        phase1(jnp.minimum(i + 1, nstep - 1))
        return carry

    lax.fori_loop(0, nstep, fused, 0)

    def phase3(n, carry):
        r0 = pl.multiple_of(n * R, R)
        z = in_ref[pl.ds(r0, R), 3 * HW:4 * HW].astype(F32)
        for h in range(H):
            lanes = slice(h * LANE, (h + 1) * LANE)
            o = o_s[pl.ds(r0, R), lanes]
            o = _rms(o, ng_ref[...]) * _silu(z[:, lanes])
            o_ref[pl.ds(r0, R), lanes] = o.astype(BF16)
        return carry

    lax.fori_loop(0, S // R, phase3, 0)


def _gdn(gdn_in, ab, abt, cw, alr, dtr, alc, dtc, ng, B, S):
    C = GDN_CHUNK
    NC = S // C
    HW = GDN_HEADS * GDN_DK
    kern = functools.partial(_gdn_kernel, S=S)
    scratch = [pltpu.VMEM((S + 16, HW), F32),
               pltpu.VMEM((S, HW), BF16), pltpu.VMEM((S, HW), BF16), pltpu.VMEM((S, HW), BF16),
               pltpu.VMEM((S, LANE), F32), pltpu.VMEM((S, LANE), F32), pltpu.VMEM((S, LANE), F32),
               pltpu.VMEM((NC, N_STREAMS, LANE), F32), pltpu.VMEM((NC, N_STREAMS, LANE), F32),
               pltpu.VMEM((NC, N_STREAMS, LANE), F32),
               pltpu.VMEM((NC * GDN_HEADS, 2 * C, 2 * C), BF16),
               pltpu.VMEM((NC * GDN_HEADS, 2 * C, 2 * C), BF16),
               pltpu.VMEM((GDN_HEADS, GDN_DK, 2 * GDN_DV), F32),
               pltpu.VMEM((S, HW), F32)]
    return pl.pallas_call(
        kern,
        grid=(B,),
        in_specs=[pl.BlockSpec((S, W_GDN), lambda b: (b, 0), pipeline_mode=pl.Buffered(1)),
                  pl.BlockSpec((S, W_AB), lambda b: (b, 0)),
                  pl.BlockSpec((16, S), lambda b: (0, b)),
                  _const_spec(cw.shape), _const_spec(alr.shape), _const_spec(dtr.shape),
                  _const_spec(alc.shape), _const_spec(dtc.shape), _const_spec(ng.shape)],
        out_specs=pl.BlockSpec((S, HW), lambda b: (b, 0)),
        out_shape=jax.ShapeDtypeStruct((B * S, HW), BF16),
        scratch_shapes=scratch,
        compiler_params=_cparams("arbitrary"),
        name="gated_deltanet",
    )(gdn_in, ab, abt, cw, alr, dtr, alc, dtc, ng)


def _merge_kernel(x_ref, mod_ref, om_ref, or_ref, og_ref, gates_ref, wb_ref, wo_ref, o_ref):
    D = D_MODEL
    mixed = None
    for n, src in enumerate((om_ref, or_ref, og_ref)):
        g = _sigmoid(gates_ref[:, n * D:(n + 1) * D].astype(F32))
        term = g * _dot(src[...], wb_ref[n])
        mixed = term if mixed is None else mixed + term
    gt1 = mod_ref[:, 2 * D:3 * D]
    o_ref[...] = x_ref[...] + gt1 * _dot(mixed.astype(BF16), wo_ref[...])


def _merge(x2d, mod, o_mla, o_rg, o_gdn, gates, wb, wo, S, tm):
    T, D = x2d.shape
    per_seq = S // tm
    row = lambda i: (i, 0)
    return pl.pallas_call(
        _merge_kernel,
        grid=(T // tm,),
        in_specs=[pl.BlockSpec((tm, D), row),
                  pl.BlockSpec((None, 1, 6 * D), lambda i: (i // per_seq, 0, 0)),
                  pl.BlockSpec((tm, 512), row), pl.BlockSpec((tm, 512), row), pl.BlockSpec((tm, 512), row),
                  pl.BlockSpec((tm, W_GATES), row),
                  _const_spec(wb.shape), _const_spec(wo.shape)],
        out_specs=pl.BlockSpec((tm, D), row),
        out_shape=jax.ShapeDtypeStruct((T, D), F32),
        compiler_params=_cparams("arbitrary"),
        name="branch_merge",
    )(x2d, mod, o_mla, o_rg, o_gdn, gates, wb, wo)


def _ffn_kernel(x_ref, xp_ref, xn_ref, mod_ref, g_ref, wu_ref, cw_ref, cb_ref, wd_ref, fg_ref,
                o_ref, *, tm, per_seq, final):
    D = D_MODEL
    i = pl.program_id(0)
    sh = mod_ref[:, 3 * D:4 * D]
    sc = mod_ref[:, 4 * D:5 * D]
    gt2 = mod_ref[:, 5 * D:6 * D]
    x = x_ref[...]
    h = (_rms(x, g_ref[...]) * (1.0 + sc) + sh).astype(BF16)
    xh = jnp.concatenate([xp_ref[...], xn_ref[...]], axis=0)
    hh = (_rms(xh, g_ref[...]) * (1.0 + sc) + sh).astype(BF16)
    has_prev = jnp.where(i % per_seq == 0, 0.0, 1.0)
    has_next = jnp.where(i % per_seq == per_seq - 1, 0.0, 1.0)
    row = lax.broadcasted_iota(jnp.int32, (tm, 2 * FFN_CK), 0)
    acc = None
    for j in range(D_FF // FFN_CK):
        w = wu_ref[j]
        up = _dot(h, w)
        uh = _dot(hh, w)
        prev = uh[7:8, :] * has_prev
        nxt = uh[8:9, :] * has_next
        dn = jnp.where(row == 0, prev, pltpu.roll(up, 1, 0))
        un = jnp.where(row == tm - 1, nxt, pltpu.roll(up, tm - 1, 0))
        cw = cw_ref[j]
        y = dn * cw[0:1, :] + up * cw[1:2, :] + un * cw[2:3, :] + cb_ref[j]
        act = _silu(y[:, FFN_CK:]) * y[:, :FFN_CK]
        part = _dot(act.astype(BF16), wd_ref[j])
        acc = part if acc is None else acc + part
    y = x + gt2 * acc
    if final:
        y = _rms(y, fg_ref[...])
    o_ref[...] = y


def _ffn(x2d, mod, ln_g, wu, cw, cb, wd, fg, S, tm, final):
    T, D = x2d.shape
    per_seq = S // tm
    hb = tm // 8
    nblk8 = T // 8
    kern = functools.partial(_ffn_kernel, tm=tm, per_seq=per_seq, final=final)
    return pl.pallas_call(
        kern,
        grid=(T // tm,),
        in_specs=[pl.BlockSpec((tm, D), lambda i: (i, 0)),
                  pl.BlockSpec((8, D), lambda i: (jnp.maximum(i * hb - 1, 0), 0)),
                  pl.BlockSpec((8, D), lambda i: (jnp.minimum((i + 1) * hb, nblk8 - 1), 0)),
                  pl.BlockSpec((None, 1, 6 * D), lambda i: (i // per_seq, 0, 0)),
                  _const_spec((1, D)),
                  _const_spec(wu.shape), _const_spec(cw.shape), _const_spec(cb.shape),
                  _const_spec(wd.shape), _const_spec((1, D))],
        out_specs=pl.BlockSpec((tm, D), lambda i: (i, 0)),
        out_shape=jax.ShapeDtypeStruct((T, D), F32),
        compiler_params=_cparams("arbitrary"),
        name="conv_glu_ffn",
    )(x2d, x2d, x2d, mod, ln_g, wu, cw, cb, wd, fg)


def _rot_cols(w):
    half = w.shape[-1] // 2
    return jnp.concatenate([-w[..., half:], w[..., :half]], axis=-1)


def _prep_layer(l, p):
    D = D_MODEL
    w = p['w_in'][l]
    o = np.cumsum([0, MLA_Q_LORA, MLA_KV_LORA + MLA_ROPE, RG_WIDTH, RG_WIDTH, 3 * 512, 512, 8, 8, 3 * D])
    kro = o[1] + MLA_KV_LORA
    w_kr = w[:, kro:o[2]]
    w1 = jnp.concatenate([
        w[:, o[0]:o[1]], w[:, o[1]:kro], w_kr, _rot_cols(w_kr), jnp.zeros((D, 64), F32),
        w[:, o[2]:o[4]],
        w[:, o[4]:o[6]],
        w[:, o[6]:o[8]], jnp.zeros((D, W_AB - 16), F32),
        w[:, o[8]:o[9]]], axis=1).astype(BF16)
    wabt = w[:, o[6]:o[8]].T.astype(BF16)

    H = MLA_HEADS
    wq = p['mla_w_uq'][l].reshape(MLA_Q_LORA, H, MLA_QK)
    rope = wq[..., MLA_NOPE:]
    wq_ext = jnp.concatenate([wq[..., :MLA_NOPE], rope, _rot_cols(rope)], axis=-1)
    wq_ext = wq_ext.reshape(MLA_Q_LORA, H * HEAD_PAD).astype(BF16)
    wkv = p['mla_w_ukv'][l].reshape(MLA_KV_LORA, H, MLA_NOPE + MLA_V)
    wk_top = jnp.concatenate([wkv[..., :MLA_NOPE], jnp.zeros((MLA_KV_LORA, H, 64), F32)], axis=-1)
    e = np.zeros((128, H, HEAD_PAD), np.float32)
    for j in range(MLA_ROPE):
        for src in (j, MLA_ROPE + j):
            e[src, :, MLA_NOPE + j] = 1.0
            e[src, :, MLA_NOPE + MLA_ROPE + j] = 1.0
    wk_ext = jnp.concatenate([wk_top, jnp.asarray(e)], axis=0).reshape(MLA_KV_LORA + 128, H * HEAD_PAD).astype(BF16)
    wv = wkv[..., MLA_NOPE:]
    z = jnp.zeros_like(wv)
    even = (np.arange(H) % 2 == 0)[None, :, None]
    wv_ext = jnp.concatenate([jnp.where(even, wv, z), jnp.where(even, z, wv)], axis=-1)
    wv_ext = wv_ext.reshape(MLA_KV_LORA, H * HEAD_PAD).astype(BF16)

    eye = jnp.eye(RG_BLOCKS, dtype=F32)
    bd = lambda t: jnp.einsum('ncd,nm->ncmd', t, eye).reshape(RG_WIDTH, RG_WIDTH)
    wg = jnp.concatenate([bd(p['rg_w_a'][l, 0]), bd(p['rg_w_i'][l, 0]),
                          bd(p['rg_w_a'][l, 1]), bd(p['rg_w_i'][l, 1])], axis=1).astype(BF16)
    bg = jnp.concatenate([p['rg_b_a'][l, 0], p['rg_b_i'][l, 0], p['rg_b_a'][l, 1], p['rg_b_i'][l, 1]])[None, :]

    pad_row = lambda v: jnp.concatenate([v.reshape(-1), jnp.zeros((LANE - N_STREAMS,), F32)])[None, :]
    bc_col = lambda v: jnp.broadcast_to(v.reshape(-1, 1), (N_STREAMS, LANE))

    nck = D_FF // FFN_CK
    pair = lambda t: jnp.concatenate([t[..., :D_FF].reshape(t.shape[:-1] + (nck, FFN_CK)),
                                      t[..., D_FF:].reshape(t.shape[:-1] + (nck, FFN_CK))], axis=-1)
    wu = jnp.moveaxis(pair(p['ffn_w_up'][l]), 1, 0).astype(BF16)
    fcw = jnp.moveaxis(pair(p['ffn_conv_w'][l]), 1, 0)
    fcb = pair(p['ffn_conv_b'][l])[:, None, :]
    wd = p['ffn_w_down'][l].reshape(nck, FFN_CK, D).astype(BF16)

    return dict(
        ln1_g=p['ln1_g'][l][None, :], w1=w1, wabt=wabt,
        gq=p['mla_q_norm_g'][l][None, :], wq=wq_ext, gkv=p['mla_kv_norm_g'][l][None, :],
        wk=wk_ext, wv=wv_ext,
        rg_cw=p['rg_conv_w'][l], rg_cb=p['rg_conv_b'][l][None, :], wg=wg, bg=bg, lam=p['rg_lam'][l],
        gdn_cw=p['gdn_conv_w'][l], alr=pad_row(p['gdn_a_log'][l]), dtr=pad_row(p['gdn_dt_bias'][l]),
        alc=bc_col(p['gdn_a_log'][l]), dtc=bc_col(p['gdn_dt_bias'][l]), ng=p['gdn_norm_g'][l][None, :],
        wb=p['w_branch'][l].astype(BF16), wo=p['w_out'][l].astype(BF16),
        ln2_g=p['ln2_g'][l][None, :], wu=wu, fcw=fcw, fcb=fcb, wd=wd)


def _rope_tables(S):
    inv = 1.0 / (ROPE_THETA ** (jnp.arange(0, MLA_ROPE, 2, dtype=F32) / MLA_ROPE))
    ang = jnp.arange(S, dtype=F32)[:, None] * inv[None, :]
    cos, sin = jnp.cos(ang), jnp.sin(ang)
    scale = MLA_QK ** -0.5
    csq = scale * jnp.concatenate([jnp.ones((S, MLA_NOPE), F32), cos, cos, sin, sin], axis=1)
    csk = jnp.concatenate([cos, cos, sin, sin, jnp.zeros((S, 64), F32)], axis=1)
    return csq, csk


def _trunk(x, mod, layers, final_g, csq, csk):
    B, S, D = x.shape
    x2d = x.reshape(B * S, D)
    tm = min(512, S)
    for l, w in enumerate(layers):
        m = mod[l].reshape(B, 1, 6 * D)
        mla_in, rg_in, gdn_in, ab, abt, gates = _in_projection(x2d, m, w['ln1_g'], w['w1'], w['wabt'], S, tm)
        o_mla = _mla(mla_in, csq, csk, w['gq'], w['wq'], w['gkv'], w['wk'], w['wv'], B, S)
        o_rg = _rglru(rg_in, w['rg_cw'], w['rg_cb'], w['wg'], w['bg'], w['lam'], B, S)
        o_gdn = _gdn(gdn_in, ab, abt, w['gdn_cw'], w['alr'], w['dtr'], w['alc'], w['dtc'], w['ng'], B, S)
        x2d = _merge(x2d, m, o_mla, o_rg, o_gdn, gates, w['wb'], w['wo'], S, tm)
        x2d = _ffn(x2d, m, w['ln2_g'], w['wu'], w['fcw'], w['fcb'], w['wd'], final_g, S, min(1024, S),
                   final=(l == len(layers) - 1))
    return x2d.reshape(B, S, D)


def kernel(x_prompt, x_sample, c_prompt, c_sample, ln1_g, w_mod, b_mod, w_in, mla_q_norm_g, mla_w_uq, mla_kv_norm_g, mla_w_ukv, rg_conv_w, rg_conv_b, rg_w_a, rg_b_a, rg_w_i, rg_b_i, rg_lam, gdn_conv_w, gdn_a_log, gdn_dt_bias, gdn_norm_g, w_branch, w_out, ln2_g, ffn_w_up, ffn_conv_w, ffn_conv_b, ffn_w_down, final_norm_g):
    p = dict(ln1_g=ln1_g, w_in=w_in, mla_q_norm_g=mla_q_norm_g, mla_w_uq=mla_w_uq,
             mla_kv_norm_g=mla_kv_norm_g, mla_w_ukv=mla_w_ukv, rg_conv_w=rg_conv_w, rg_conv_b=rg_conv_b,
             rg_w_a=rg_w_a, rg_b_a=rg_b_a, rg_w_i=rg_w_i, rg_b_i=rg_b_i, rg_lam=rg_lam,
             gdn_conv_w=gdn_conv_w, gdn_a_log=gdn_a_log, gdn_dt_bias=gdn_dt_bias, gdn_norm_g=gdn_norm_g,
             w_branch=w_branch, w_out=w_out, ln2_g=ln2_g, ffn_w_up=ffn_w_up, ffn_conv_w=ffn_conv_w,
             ffn_conv_b=ffn_conv_b, ffn_w_down=ffn_w_down)
    L = w_in.shape[0]
    layers = [_prep_layer(l, p) for l in range(L)]
    Bp = x_prompt.shape[0]
    mod = _modulation(jnp.concatenate([c_prompt, c_sample], axis=0), w_mod, b_mod)
    fg = final_norm_g[None, :]
    outs = []
    for x, m in ((x_prompt, mod[:, :Bp]), (x_sample, mod[:, Bp:])):
        csq, csk = _rope_tables(x.shape[1])
        outs.append(_trunk(x, m, layers, fg, csq, csk))
    return tuple(outs)
```

```python
import functools
import math

import jax
import jax.numpy as jnp
import numpy as np
from jax import lax
from jax.experimental import pallas as pl
from jax.experimental.pallas import tpu as pltpu

D_MODEL = 1024
DEPTH = 2
MLA_HEADS = 8
MLA_Q_LORA = 384
MLA_KV_LORA = 256
MLA_NOPE = 64
MLA_ROPE = 32
MLA_V = 64
MLA_QK = MLA_NOPE + MLA_ROPE
ROPE_THETA = 10000.0
RG_WIDTH = 512
RG_BLOCKS = 8
RG_C = 8.0
GDN_HEADS = 4
GDN_DK = 128
GDN_DV = 128
GDN_CHUNK = 64
D_FF = 2816
EPS = 1e-6

LANE = 128
HEAD_PAD = 128
VMEM_LIMIT = 56 * 1024 * 1024
FFN_CK = 256
N_STREAMS = 2 * GDN_HEADS
GDN_PAIRS_PER_STEP = 4

F32 = jnp.float32
BF16 = jnp.bfloat16


def _cparams(*sem):
    return pltpu.CompilerParams(dimension_semantics=sem, vmem_limit_bytes=VMEM_LIMIT)


def _const_spec(shape):
    nd = len(shape)
    return pl.BlockSpec(shape, lambda *_: (0,) * nd, pipeline_mode=pl.Buffered(1))


def _sigmoid(x):
    return 0.5 * jnp.tanh(0.5 * x) + 0.5


def _silu(x):
    return x * _sigmoid(x)


def _softplus(x):
    return jnp.maximum(x, 0.0) + jnp.log(1.0 + jnp.exp(-jnp.abs(x)))


def _gelu_tanh(x):
    c = math.sqrt(2.0 / math.pi)
    return 0.5 * x * (1.0 + jnp.tanh(c * (x + 0.044715 * (x * x * x))))


def _rms(x, g):
    return x * lax.rsqrt(jnp.mean(x * x, axis=-1, keepdims=True) + EPS) * g


def _dot(a, b):
    return jnp.dot(a, b, preferred_element_type=F32)


def _dot_nt(a, b):
    return lax.dot_general(a, b, (((1,), (1,)), ((), ())), preferred_element_type=F32)


def _dot_tn(a, b):
    return lax.dot_general(a, b, (((0,), (0,)), ((), ())), preferred_element_type=F32)


def _split3(x):
    x1 = x.astype(BF16)
    r1 = x - x1.astype(F32)
    x2 = r1.astype(BF16)
    x3 = (r1 - x2.astype(F32)).astype(BF16)
    return x1, x2, x3


def _mod_kernel(c_ref, w_ref, b_ref, o_ref):
    c = c_ref[...]
    a1, a2, a3 = _split3(_silu(c))
    w1, w2, w3 = _split3(w_ref[...])
    acc = _dot(a1, w1) + (_dot(a1, w2) + _dot(a2, w1)) + (_dot(a1, w3) + _dot(a2, w2) + _dot(a3, w1))
    o_ref[...] = acc + b_ref[...]


def _modulation(c, w_mod, b_mod):
    L, D, N = w_mod.shape
    Bt = c.shape[0]
    tn = 1024
    return pl.pallas_call(
        _mod_kernel,
        grid=(L, N // tn),
        in_specs=[pl.BlockSpec((Bt, D), lambda l, j: (0, 0)),
                  pl.BlockSpec((None, D, tn), lambda l, j: (l, 0, j)),
                  pl.BlockSpec((None, 1, tn), lambda l, j: (l, 0, j))],
        out_specs=pl.BlockSpec((None, Bt, tn), lambda l, j: (l, 0, j)),
        out_shape=jax.ShapeDtypeStruct((L, Bt, N), F32),
        compiler_params=_cparams("arbitrary", "arbitrary"),
        name="modulation",
    )(c, w_mod, b_mod.reshape(L, 1, N))


W_MLA, W_RG, W_GDN, W_AB, W_GATES = 768, 1024, 2048, 128, 3 * D_MODEL
_IN_OFFS = np.cumsum([0, W_MLA, W_RG, W_GDN, W_AB, W_GATES])


def _inproj_kernel(x_ref, mod_ref, g_ref, w_ref, wabt_ref,
                   mla_ref, rg_ref, gdn_ref, ab_ref, abt_ref, gates_ref):
    D = D_MODEL
    x = x_ref[...]
    sh = mod_ref[:, 0:D]
    sc = mod_ref[:, D:2 * D]
    h = (_rms(x, g_ref[...]) * (1.0 + sc) + sh).astype(BF16)
    o = _IN_OFFS
    mla_ref[...] = _dot(h, w_ref[:, o[0]:o[1]]).astype(BF16)
    rg_ref[...] = _dot(h, w_ref[:, o[1]:o[2]]).astype(BF16)
    gdn_ref[...] = _dot(h, w_ref[:, o[2]:o[3]]).astype(BF16)
    ab_ref[...] = _dot(h, w_ref[:, o[3]:o[4]])
    gates_ref[...] = _dot(h, w_ref[:, o[4]:o[5]]).astype(BF16)
    abt_ref[...] = _dot_nt(wabt_ref[...], h)


def _in_projection(x2d, mod, ln_g, w1, wabt, S, tm):
    T, D = x2d.shape
    per_seq = S // tm
    row = lambda i: (i, 0)
    outs = [(W_MLA, BF16), (W_RG, BF16), (W_GDN, BF16), (W_AB, F32)]
    out_shape = [jax.ShapeDtypeStruct((T, w), dt) for w, dt in outs]
    out_specs = [pl.BlockSpec((tm, w), row) for w, _ in outs]
    out_shape += [jax.ShapeDtypeStruct((16, T), F32), jax.ShapeDtypeStruct((T, W_GATES), BF16)]
    out_specs += [pl.BlockSpec((16, tm), lambda i: (0, i)), pl.BlockSpec((tm, W_GATES), row)]
    return pl.pallas_call(
        _inproj_kernel,
        grid=(T // tm,),
        in_specs=[pl.BlockSpec((tm, D), row),
                  pl.BlockSpec((None, 1, 6 * D), lambda i: (i // per_seq, 0, 0)),
                  _const_spec((1, D)),
                  _const_spec(w1.shape),
                  _const_spec(wabt.shape)],
        out_specs=out_specs,
        out_shape=out_shape,
        compiler_params=_cparams("arbitrary"),
        name="in_projection",
    )(x2d, mod, ln_g, w1, wabt)


def _mla_kernel(in_ref, csq_ref, csk_ref, gq_ref, wq_ref, gkv_ref, wk_ref, wv_ref,
                o_ref, q_s, k_s, v_s, *, S, tq, tr):
    hp = pl.program_id(1)

    @pl.when(hp == 0)
    def _project():
        for r in range(S // tr):
            rows = slice(r * tr, (r + 1) * tr)
            qd = in_ref[rows, 0:MLA_Q_LORA].astype(F32)
            ckv = in_ref[rows, MLA_Q_LORA:MLA_Q_LORA + MLA_KV_LORA].astype(F32)
            kr = in_ref[rows, 640:768].astype(F32) * csk_ref[rows, :]
            cq = _rms(qd, gq_ref[...]).astype(BF16)
            ckvn = _rms(ckv, gkv_ref[...]).astype(BF16)
            q = _dot(cq, wq_ref[...])
            kin = jnp.concatenate([ckvn, kr.astype(BF16)], axis=1)
            k = _dot(kin, wk_ref[...])
            v = _dot(ckvn, wv_ref[...])
            cs = csq_ref[rows, :]
            for p in range(MLA_HEADS // 2):
                for hh in range(2):
                    h = 2 * p + hh
                    q_s[p, rows, hh * LANE:(hh + 1) * LANE] = (
                        q[:, h * LANE:(h + 1) * LANE] * cs).astype(BF16)
                k_s[p, rows, :] = k[:, 2 * p * LANE:(2 * p + 2) * LANE].astype(BF16)
                v_s[p, rows, :] = v[:, 2 * p * LANE:(2 * p + 2) * LANE].astype(BF16)

    for qt in range(S // tq):
        rows = slice(qt * tq, (qt + 1) * tq)
        acc = None
        for hh in range(2):
            lanes = slice(hh * LANE, (hh + 1) * LANE)
            q = q_s[hp, rows, lanes]
            k = k_s[hp, :, lanes]
            s = _dot_nt(q, k)
            m = jnp.max(s, axis=-1, keepdims=True)
            p = jnp.exp(s - m)
            l = jnp.sum(p, axis=-1, keepdims=True)
            o = _dot(p.astype(BF16), v_s[hp, :, lanes]) * (1.0 / l)
            acc = o if acc is None else acc + o
        o_ref[rows, :] = acc.astype(BF16)


def _mla(mla_in, csq, csk, gq, wq, gkv, wk, wv, B, S):
    tq = min(512, S)
    tr = min(512, S)
    npair = MLA_HEADS // 2
    kern = functools.partial(_mla_kernel, S=S, tq=tq, tr=tr)
    return pl.pallas_call(
        kern,
        grid=(B, npair),
        in_specs=[pl.BlockSpec((S, W_MLA), lambda b, p: (b, 0)),
                  _const_spec(csq.shape), _const_spec(csk.shape),
                  _const_spec(gq.shape), _const_spec(wq.shape),
                  _const_spec(gkv.shape), _const_spec(wk.shape), _const_spec(wv.shape)],
        out_specs=pl.BlockSpec((S, LANE), lambda b, p: (b, p)),
        out_shape=jax.ShapeDtypeStruct((B * S, MLA_HEADS * MLA_V), BF16),
        scratch_shapes=[pltpu.VMEM((npair, S, 2 * LANE), BF16),
                        pltpu.VMEM((npair, S, 2 * LANE), BF16),
                        pltpu.VMEM((npair, S, 2 * LANE), BF16)],
        compiler_params=_cparams("arbitrary", "arbitrary"),
        name="mla_attention",
    )(mla_in, csq, csk, gq, wq, gkv, wk, wv)


def _conv4_window(xw, w_ref, R):
    acc = xw[7:7 + R] * w_ref[0:1, :]
    acc = acc + xw[8:8 + R] * w_ref[1:2, :]
    acc = acc + xw[9:9 + R] * w_ref[2:3, :]
    acc = acc + xw[10:10 + R] * w_ref[3:4, :]
    return acc


def _scan_rows(a, b, carry, reverse):
    R, W = a.shape
    a = a.reshape(R // 8, 8, W)
    b = b.reshape(R // 8, 8, W)
    sub = lax.broadcasted_iota(jnp.int32, a.shape, 1)
    for d in (1, 2, 4):
        shift, ok = (8 - d, sub < 8 - d) if reverse else (d, sub >= d)
        a_sh = pltpu.roll(a, shift, 1)
        b_sh = pltpu.roll(b, shift, 1)
        b = b + a * jnp.where(ok, b_sh, 0.0)
        a = a * jnp.where(ok, a_sh, 1.0)
    tiles = [None] * (R // 8)
    for j in (range(R // 8 - 1, -1, -1) if reverse else range(R // 8)):
        t = b[j] + a[j] * carry
        tiles[j] = t
        carry = t[0:1] if reverse else t[7:8]
    return jnp.concatenate(tiles, axis=0), carry


def _rglru_kernel(in_ref, cw_ref, cb_ref, wg_ref, bg_ref, lam_ref, o_ref,
                  xpad, hf_s, xc_s, carry_s, *, S, R):
    W = RG_WIDTH
    nchunk = S // R
    xpad[0:8, :] = jnp.zeros((8, W), F32)
    xpad[S + 8:S + 16, :] = jnp.zeros((8, W), F32)
    xpad[8:S + 8, :] = in_ref[:, 0:W].astype(F32)
    nsp = -RG_C * _softplus(-lam_ref[...])

    def direction(d, n):
        r0 = pl.multiple_of(n * R, R)
        if d == 0:
            xc = _conv4_window(xpad[pl.ds(r0, R + 16), :], cw_ref, R) + cb_ref[...]
            xc_s[pl.ds(r0, R), :] = xc
        else:
            xc = xc_s[pl.ds(r0, R), :]
        gates = _dot(xc.astype(BF16), wg_ref[:, 2 * d * W:(2 * d + 2) * W]) \
            + bg_ref[:, 2 * d * W:(2 * d + 2) * W]
        r = _sigmoid(gates[:, :W])
        i = _sigmoid(gates[:, W:])
        log_a = nsp[d:d + 1, :] * r
        a = jnp.exp(log_a)
        th = jnp.tanh(log_a)
        m2 = -2.0 * th / (1.0 - th)
        mult = jnp.where(m2 > 0.0, m2 * lax.rsqrt(m2), 0.0)
        t = r0 + lax.broadcasted_iota(jnp.int32, (R, W), 0)
        first = t == (S - 1 if d == 1 else 0)
        mult = jnp.where(first, 1.0, mult)
        h, carry = _scan_rows(a, mult * (i * xc), carry_s[...], reverse=(d == 1))
        carry_s[...] = carry
        return r0, h

    carry_s[...] = jnp.zeros((1, W), F32)

    def fwd(n, c):
        r0, h = direction(0, n)
        hf_s[pl.ds(r0, R), :] = h
        return c

    lax.fori_loop(0, nchunk, fwd, 0)
    carry_s[...] = jnp.zeros((1, W), F32)

    def bwd(n, c):
        r0, h = direction(1, nchunk - 1 - n)
        gate = in_ref[pl.ds(r0, R), W:2 * W].astype(F32)
        o_ref[pl.ds(r0, R), :] = ((hf_s[pl.ds(r0, R), :] + h) * _gelu_tanh(gate)).astype(BF16)
        return c

    lax.fori_loop(0, nchunk, bwd, 0)


def _rglru(rg_in, cw, cb, wg, bg, lam, B, S):
    R = min(256, S)
    kern = functools.partial(_rglru_kernel, S=S, R=R)
    return pl.pallas_call(
        kern,
        grid=(B,),
        in_specs=[pl.BlockSpec((S, W_RG), lambda b: (b, 0)),
                  _const_spec(cw.shape), _const_spec(cb.shape), _const_spec(wg.shape),
                  _const_spec(bg.shape), _const_spec(lam.shape)],
        out_specs=pl.BlockSpec((S, RG_WIDTH), lambda b: (b, 0)),
        out_shape=jax.ShapeDtypeStruct((B * S, RG_WIDTH), BF16),
        scratch_shapes=[pltpu.VMEM((S + 16, RG_WIDTH), F32),
                        pltpu.VMEM((S, RG_WIDTH), F32),
                        pltpu.VMEM((S, RG_WIDTH), F32),
                        pltpu.VMEM((1, RG_WIDTH), F32)],
        compiler_params=_cparams("arbitrary"),
        name="rglru",
    )(rg_in, cw, cb, wg, bg, lam)


def _tri_consts():
    C = GDN_CHUNK
    r = lax.broadcasted_iota(jnp.int32, (2 * C, 2 * C), 0)
    c = lax.broadcasted_iota(jnp.int32, (2 * C, 2 * C), 1)
    same = (r // C) == (c // C)
    one = lambda m: jnp.where(m, 1.0, 0.0).astype(BF16)
    return r, c, same, one


def _unit_tri_inverse(a_list, ri, ci, eye, width):
    blk = lambda n: (ri // n) == (ci // n)
    ident = jnp.where(eye, 1.0, 0.0)
    ps = [jnp.where(blk(8), -a, 0.0).astype(BF16) for a in a_list]
    ts = [ident + p.astype(F32) for p in ps]
    for _ in range(2):
        ps = [_dot(p, p).astype(BF16) for p in ps]
        ts = [t + _dot(t.astype(BF16), p) for t, p in zip(ts, ps)]
    n = 8
    while n < width:
        sel = blk(2 * n) & jnp.logical_not(blk(n))
        offs = [jnp.where(sel, a, 0.0).astype(BF16) for a in a_list]
        tbs = [t.astype(BF16) for t in ts]
        mids = [_dot(tb, off).astype(BF16) for tb, off in zip(tbs, offs)]
        ts = [t - _dot(mid, tb) for t, mid, tb in zip(ts, mids, tbs)]
        n *= 2
    return ts


def _gdn_kernel(in_ref, ab_ref, abt_ref, cw_ref, alr_ref, dtr_ref, alc_ref, dtc_ref, ng_ref,
                o_ref, xpad, qn_s, kn_s, vn_s, gcol_s, rcol_s, bcol_s, grow_s, rrow_s, brow_s,
                m1_s, m2_s, st_s, o_s, *, S):
    C = GDN_CHUNK
    H = GDN_HEADS
    NC = S // C
    HW = H * GDN_DK
    R = min(256, S)

    xpad[0:8, :] = jnp.zeros((8, HW), F32)
    xpad[S + 8:S + 16, :] = jnp.zeros((8, HW), F32)
    for grp, dst in enumerate((qn_s, kn_s, vn_s)):
        xpad[8:S + 8, :] = in_ref[:, grp * HW:(grp + 1) * HW].astype(F32)

        def conv_body(n, c, grp=grp, dst=dst):
            r0 = pl.multiple_of(n * R, R)
            xw = xpad[pl.ds(r0, R + 16), :]
            y = _silu(_conv4_window(xw, cw_ref.at[:, grp * HW:(grp + 1) * HW], R))
            if grp < 2:
                for h in range(H):
                    t = y[:, h * LANE:(h + 1) * LANE]
                    t = t * lax.rsqrt(jnp.sum(t * t, axis=-1, keepdims=True) + EPS)
                    dst[pl.ds(r0, R), h * LANE:(h + 1) * LANE] = t.astype(BF16)
            else:
                dst[pl.ds(r0, R), :] = y.astype(BF16)
            return c

        lax.fori_loop(0, S // R, conv_body, 0)

    r, c, same, one = _tri_consts()
    lane = lax.broadcasted_iota(jnp.int32, (2 * C, LANE), 1)
    is_fwd_lane = lane < H
    sub = lax.broadcasted_iota(jnp.int32, (N_STREAMS, LANE), 0)
    is_fwd_sub = sub < H
    tl_incl = one(same & (c <= r))
    tu_incl = one(same & (c >= r))
    tl_excl = one(same & (c < r))
    tu_excl = one(same & (c > r))

    def sum3(parts, mat, left):
        acc = None
        for p_ in parts:
            t = _dot(mat, p_) if left else _dot(p_, mat)
            acc = t if acc is None else acc + t
        return acc

    for pr in range(S // (2 * C)):
        rows = slice(pr * 2 * C, (pr + 1) * 2 * C)
        ab = ab_ref[rows, :]
        g_col = -jnp.exp(alr_ref[...]) * _softplus(ab + dtr_ref[...])
        parts = _split3(g_col)
        gcol_s[rows, :] = jnp.where(is_fwd_lane, sum3(parts, tl_incl, True), sum3(parts, tu_incl, True))
        rcol_s[rows, :] = jnp.where(is_fwd_lane, sum3(parts, tu_excl, True), sum3(parts, tl_excl, True))
        bcol_s[rows, :] = _sigmoid(ab)
        abt = abt_ref[:, rows]
        g_row = -jnp.exp(alc_ref[...]) * _softplus(abt[0:N_STREAMS, :] + dtc_ref[...])
        b_row = _sigmoid(abt[N_STREAMS:2 * N_STREAMS, :])
        parts = _split3(g_row)
        for half in range(2):
            in_half = (r // C) == half
            cj = c % C
            rj = r % C
            incl_f = one(in_half & (rj <= cj))
            incl_b = one(in_half & (rj >= cj))
            excl_f = one(in_half & (rj > cj))
            excl_b = one(in_half & (rj < cj))
            ch = pr * 2 + half
            grow_s[ch] = jnp.where(is_fwd_sub, sum3(parts, incl_f, False), sum3(parts, incl_b, False))
            rrow_s[ch] = jnp.where(is_fwd_sub, sum3(parts, excl_f, False), sum3(parts, excl_b, False))
            dup = one(in_half & (rj == cj))
            brow_s[ch] = _dot(b_row.astype(BF16), dup) + _dot((b_row - b_row.astype(BF16).astype(F32)).astype(BF16), dup)

    P = 2 * C
    ri = lax.broadcasted_iota(jnp.int32, (P, P), 0)
    ci = lax.broadcasted_iota(jnp.int32, (P, P), 1)
    eye = ri == ci
    top = ri < C
    same_dir = (ri // C) == (ci // C)
    bot = jnp.logical_not(top)
    incl = same_dir & ((top & (ci <= ri)) | (bot & (ci >= ri)))
    strict = same_dir & ((top & (ci < ri)) | (bot & (ci > ri)))
    lane_f = lax.broadcasted_iota(jnp.int32, (H, LANE), 1) < C
    scale = GDN_DK ** -0.5

    def pair_rows(ref, rf, rb, lanes):
        return jnp.concatenate([ref[pl.ds(rf, C), lanes], ref[pl.ds(rb, C), lanes]], axis=0)

    def pair_lanes(arr_s, n, nb):
        return jnp.where(lane_f, arr_s[n][0:H], arr_s[nb][H:2 * H])

    def col_pair(col_f, col_b, lane_idx_f, lane_idx_b):
        return jnp.concatenate([jnp.broadcast_to(col_f[:, lane_idx_f:lane_idx_f + 1], (C, P)),
                                jnp.broadcast_to(col_b[:, lane_idx_b:lane_idx_b + 1], (C, P))], axis=0)

    G = GDN_PAIRS_PER_STEP

    def phase1(i):
        a_l, decay_l, qk_l, brow_l = [], [], [], []
        for n in [G * i + g for g in range(G)]:
            nb = NC - 1 - n
            rf = pl.multiple_of(n * C, C)
            rb = pl.multiple_of(nb * C, C)
            gcol_f, gcol_b = gcol_s[pl.ds(rf, C), :], gcol_s[pl.ds(rb, C), :]
            bcol_f, bcol_b = bcol_s[pl.ds(rf, C), :], bcol_s[pl.ds(rb, C), :]
            grow = pair_lanes(grow_s, n, nb)
            brow = pair_lanes(brow_s, n, nb)
            for h in range(H):
                lanes = slice(h * LANE, (h + 1) * LANE)
                kp = pair_rows(kn_s, rf, rb, lanes)
                qp = pair_rows(qn_s, rf, rb, lanes)
                kq = _dot_nt(jnp.concatenate([kp, qp], axis=0), kp)
                gi = col_pair(gcol_f, gcol_b, h, H + h)
                bi = col_pair(bcol_f, bcol_b, N_STREAMS + h, N_STREAMS + H + h)
                decay = jnp.where(incl, jnp.exp(jnp.minimum(gi - grow[h:h + 1, :], 0.0)), 0.0)
                a_l.append(jnp.where(strict, bi * kq[0:P] * decay, 0.0))
                decay_l.append(decay)
                qk_l.append(kq[P:2 * P])
                brow_l.append(brow[h:h + 1, :])
        t_l = _unit_tri_inverse(a_l, ri, ci, eye, C)
        for j in range(G * H):
            m1_s[G * i * H + j] = (t_l[j] * brow_l[j]).astype(BF16)
            m2_s[G * i * H + j] = (qk_l[j] * decay_l[j] * scale).astype(BF16)

    st_s[...] = jnp.zeros(st_s.shape, F32)
    o_s[...] = jnp.zeros(o_s.shape, F32)

    def phase2(n):
        nb = NC - 1 - n
        rf = pl.multiple_of(n * C, C)
        rb = pl.multiple_of(nb * C, C)
        eg_f, eg_b = jnp.exp(gcol_s[pl.ds(rf, C), :]), jnp.exp(gcol_s[pl.ds(rb, C), :])
        ek_f, ek_b = jnp.exp(rcol_s[pl.ds(rf, C), :]), jnp.exp(rcol_s[pl.ds(rb, C), :])
        etot_f = jnp.exp(grow_s[n] + rrow_s[n])
        etot_b = jnp.exp(grow_s[nb] + rrow_s[nb])
        kp_l, kqs_l, vp_l = [], [], []
        for h in range(H):
            lanes = slice(h * LANE, (h + 1) * LANE)
            kp = pair_rows(kn_s, rf, rb, lanes)
            qp = pair_rows(qn_s, rf, rb, lanes)
            kp_l.append(kp)
            vp_l.append(pair_rows(vn_s, rf, rb, lanes))
            kqs_l.append(_dot(jnp.concatenate([kp, qp], axis=0), st_s[h].astype(BF16)))
        vnew_l, oq_l = [], []
        for h in range(H):
            kqs = kqs_l[h]
            ks = jnp.concatenate([kqs[0:C, 0:LANE], kqs[C:P, LANE:2 * LANE]], axis=0)
            qs = jnp.concatenate([kqs[P:P + C, 0:LANE], kqs[P + C:2 * P, LANE:2 * LANE]], axis=0)
            egc = col_pair(eg_f, eg_b, h, H + h)
            oq_l.append(egc * scale * qs)
            r1 = (vp_l[h].astype(F32) - egc * ks).astype(BF16)
            vnew_l.append(_dot(m1_s[n * H + h], r1))
        for h in range(H):
            lanes = slice(h * LANE, (h + 1) * LANE)
            v_new = vnew_l[h]
            o = _dot(m2_s[n * H + h], v_new.astype(BF16)) + oq_l[h]
            x = col_pair(ek_f, ek_b, h, H + h) * v_new
            xbd = jnp.concatenate([jnp.where(top, x, 0.0), jnp.where(top, 0.0, x)], axis=1).astype(BF16)
            etot = jnp.concatenate([etot_f[h:h + 1, :], etot_b[H + h:H + h + 1, :]], axis=1)
            st_s[h] = st_s[h] * etot + _dot_tn(kp_l[h], xbd)
            o_s[pl.ds(rf, C), lanes] += o[0:C]
            o_s[pl.ds(rb, C), lanes] += o[C:P]

    nstep = NC // G
    phase1(0)

    def fused(i, carry):
        for g in range(G):
            phase2(G * i + g)
        phase1(jnp.minimum(i + 1, nstep - 1))
        return carry

    lax.fori_loop(0, nstep, fused, 0)

    def phase3(n, carry):
        r0 = pl.multiple_of(n * R, R)
        z = in_ref[pl.ds(r0, R), 3 * HW:4 * HW].astype(F32)
        for h in range(H):
            lanes = slice(h * LANE, (h + 1) * LANE)
            o = o_s[pl.ds(r0, R), lanes]
            o = _rms(o, ng_ref[...]) * _silu(z[:, lanes])
            o_ref[pl.ds(r0, R), lanes] = o.astype(BF16)
        return carry

    lax.fori_loop(0, S // R, phase3, 0)


def _gdn(gdn_in, ab, abt, cw, alr, dtr, alc, dtc, ng, B, S):
    C = GDN_CHUNK
    NC = S // C
    HW = GDN_HEADS * GDN_DK
    kern = functools.partial(_gdn_kernel, S=S)
    scratch = [pltpu.VMEM((S + 16, HW), F32),
               pltpu.VMEM((S, HW), BF16), pltpu.VMEM((S, HW), BF16), pltpu.VMEM((S, HW), BF16),
               pltpu.VMEM((S, LANE), F32), pltpu.VMEM((S, LANE), F32), pltpu.VMEM((S, LANE), F32),
               pltpu.VMEM((NC, N_STREAMS, LANE), F32), pltpu.VMEM((NC, N_STREAMS, LANE), F32),
               pltpu.VMEM((NC, N_STREAMS, LANE), F32),
               pltpu.VMEM((NC * GDN_HEADS, 2 * C, 2 * C), BF16),
               pltpu.VMEM((NC * GDN_HEADS, 2 * C, 2 * C), BF16),
               pltpu.VMEM((GDN_HEADS, GDN_DK, 2 * GDN_DV), F32),
               pltpu.VMEM((S, HW), F32)]
    return pl.pallas_call(
        kern,
        grid=(B,),
        in_specs=[pl.BlockSpec((S, W_GDN), lambda b: (b, 0), pipeline_mode=pl.Buffered(1)),
                  pl.BlockSpec((S, W_AB), lambda b: (b, 0)),
                  pl.BlockSpec((16, S), lambda b: (0, b)),
                  _const_spec(cw.shape), _const_spec(alr.shape), _const_spec(dtr.shape),
                  _const_spec(alc.shape), _const_spec(dtc.shape), _const_spec(ng.shape)],
        out_specs=pl.BlockSpec((S, HW), lambda b: (b, 0)),
        out_shape=jax.ShapeDtypeStruct((B * S, HW), BF16),
        scratch_shapes=scratch,
        compiler_params=_cparams("arbitrary"),
        name="gated_deltanet",
    )(gdn_in, ab, abt, cw, alr, dtr, alc, dtc, ng)


def _merge_kernel(x_ref, mod_ref, om_ref, or_ref, og_ref, gates_ref, wb_ref, wo_ref, o_ref):
    D = D_MODEL
    mixed = None
    for n, src in enumerate((om_ref, or_ref, og_ref)):
        g = _sigmoid(gates_ref[:, n * D:(n + 1) * D].astype(F32))
        term = g * _dot(src[...], wb_ref[n])
        mixed = term if mixed is None else mixed + term
    gt1 = mod_ref[:, 2 * D:3 * D]
    o_ref[...] = x_ref[...] + gt1 * _dot(mixed.astype(BF16), wo_ref[...])


def _merge(x2d, mod, o_mla, o_rg, o_gdn, gates, wb, wo, S, tm):
    T, D = x2d.shape
    per_seq = S // tm
    row = lambda i: (i, 0)
    return pl.pallas_call(
        _merge_kernel,
        grid=(T // tm,),
        in_specs=[pl.BlockSpec((tm, D), row),
                  pl.BlockSpec((None, 1, 6 * D), lambda i: (i // per_seq, 0, 0)),
                  pl.BlockSpec((tm, 512), row), pl.BlockSpec((tm, 512), row), pl.BlockSpec((tm, 512), row),
                  pl.BlockSpec((tm, W_GATES), row),
                  _const_spec(wb.shape), _const_spec(wo.shape)],
        out_specs=pl.BlockSpec((tm, D), row),
        out_shape=jax.ShapeDtypeStruct((T, D), F32),
        compiler_params=_cparams("arbitrary"),
        name="branch_merge",
    )(x2d, mod, o_mla, o_rg, o_gdn, gates, wb, wo)


def _ffn_kernel(x_ref, xp_ref, xn_ref, mod_ref, g_ref, wu_ref, cw_ref, cb_ref, wd_ref, fg_ref,
                o_ref, act_s, *, tm, per_seq, final):
    D = D_MODEL
    i = pl.program_id(0)
    sh = mod_ref[:, 3 * D:4 * D]
    sc = mod_ref[:, 4 * D:5 * D]
    gt2 = mod_ref[:, 5 * D:6 * D]
    x = x_ref[...]
    h = (_rms(x, g_ref[...]) * (1.0 + sc) + sh).astype(BF16)
    xh = jnp.concatenate([xp_ref[...], xn_ref[...]], axis=0)
    hh = (_rms(xh, g_ref[...]) * (1.0 + sc) + sh).astype(BF16)
    has_prev = jnp.where(i % per_seq == 0, 0.0, 1.0)
    has_next = jnp.where(i % per_seq == per_seq - 1, 0.0, 1.0)
    row8 = lax.broadcasted_iota(jnp.int32, (8, 2 * FFN_CK), 0)
    nck = D_FF // FFN_CK
    split = (nck + 1) // 2
    acc = None
    for j in range(nck):
        w = wu_ref[j]
        up = _dot(h, w)
        uh = _dot(hh, w)
        prev = uh[7:8, :] * has_prev
        nxt = uh[8:9, :] * has_next
        dn = pltpu.roll(up, 1, 0)
        dn = jnp.concatenate([jnp.where(row8 == 0, prev, dn[0:8]), dn[8:]], axis=0)
        un = pltpu.roll(up, tm - 1, 0)
        un = jnp.concatenate([un[:tm - 8], jnp.where(row8 == 7, nxt, un[tm - 8:])], axis=0)
        cw = cw_ref[j]
        y = dn * cw[0:1, :] + up * cw[1:2, :] + un * cw[2:3, :] + cb_ref[j]
        act = _silu(y[:, FFN_CK:]) * y[:, :FFN_CK]
        act_s[:, j * FFN_CK:(j + 1) * FFN_CK] = act.astype(BF16)
        if j == split - 1:
            acc = _dot(act_s[:, 0:split * FFN_CK], wd_ref[0:split * FFN_CK, :])
    acc = acc + _dot(act_s[:, split * FFN_CK:], wd_ref[split * FFN_CK:, :])
    y = x + gt2 * acc
    if final:
        y = _rms(y, fg_ref[...])
    o_ref[...] = y


def _ffn(x2d, mod, ln_g, wu, cw, cb, wd, fg, S, tm, final):
    T, D = x2d.shape
    per_seq = S // tm
    hb = tm // 8
    nblk8 = T // 8
    kern = functools.partial(_ffn_kernel, tm=tm, per_seq=per_seq, final=final)
    return pl.pallas_call(
        kern,
        grid=(T // tm,),
        in_specs=[pl.BlockSpec((tm, D), lambda i: (i, 0)),
                  pl.BlockSpec((8, D), lambda i: (jnp.maximum(i * hb - 1, 0), 0)),
                  pl.BlockSpec((8, D), lambda i: (jnp.minimum((i + 1) * hb, nblk8 - 1), 0)),
                  pl.BlockSpec((None, 1, 6 * D), lambda i: (i // per_seq, 0, 0)),
                  _const_spec((1, D)),
                  _const_spec(wu.shape), _const_spec(cw.shape), _const_spec(cb.shape),
                  _const_spec(wd.shape), _const_spec((1, D))],
        out_specs=pl.BlockSpec((tm, D), lambda i: (i, 0)),
        out_shape=jax.ShapeDtypeStruct((T, D), F32),
        scratch_shapes=[pltpu.VMEM((tm, D_FF), BF16)],
        compiler_params=_cparams("arbitrary"),
        name="conv_glu_ffn",
    )(x2d, x2d, x2d, mod, ln_g, wu, cw, cb, wd, fg)


def _rot_cols(w):
    half = w.shape[-1] // 2
    return jnp.concatenate([-w[..., half:], w[..., :half]], axis=-1)


def _prep_layer(l, p):
    D = D_MODEL
    w = p['w_in'][l]
    o = np.cumsum([0, MLA_Q_LORA, MLA_KV_LORA + MLA_ROPE, RG_WIDTH, RG_WIDTH, 3 * 512, 512, 8, 8, 3 * D])
    kro = o[1] + MLA_KV_LORA
    w_kr = w[:, kro:o[2]]
    w1 = jnp.concatenate([
        w[:, o[0]:o[1]], w[:, o[1]:kro], w_kr, _rot_cols(w_kr), jnp.zeros((D, 64), F32),
        w[:, o[2]:o[4]],
        w[:, o[4]:o[6]],
        w[:, o[6]:o[8]], jnp.zeros((D, W_AB - 16), F32),
        w[:, o[8]:o[9]]], axis=1).astype(BF16)
    wabt = w[:, o[6]:o[8]].T.astype(BF16)

    H = MLA_HEADS
    wq = p['mla_w_uq'][l].reshape(MLA_Q_LORA, H, MLA_QK)
    rope = wq[..., MLA_NOPE:]
    wq_ext = jnp.concatenate([wq[..., :MLA_NOPE], rope, _rot_cols(rope)], axis=-1)
    wq_ext = wq_ext.reshape(MLA_Q_LORA, H * HEAD_PAD).astype(BF16)
    wkv = p['mla_w_ukv'][l].reshape(MLA_KV_LORA, H, MLA_NOPE + MLA_V)
    wk_top = jnp.concatenate([wkv[..., :MLA_NOPE], jnp.zeros((MLA_KV_LORA, H, 64), F32)], axis=-1)
    e = np.zeros((128, H, HEAD_PAD), np.float32)
    for j in range(MLA_ROPE):
        for src in (j, MLA_ROPE + j):
            e[src, :, MLA_NOPE + j] = 1.0
            e[src, :, MLA_NOPE + MLA_ROPE + j] = 1.0
    wk_ext = jnp.concatenate([wk_top, jnp.asarray(e)], axis=0).reshape(MLA_KV_LORA + 128, H * HEAD_PAD).astype(BF16)
    wv = wkv[..., MLA_NOPE:]
    z = jnp.zeros_like(wv)
    even = (np.arange(H) % 2 == 0)[None, :, None]
    wv_ext = jnp.concatenate([jnp.where(even, wv, z), jnp.where(even, z, wv)], axis=-1)
    wv_ext = wv_ext.reshape(MLA_KV_LORA, H * HEAD_PAD).astype(BF16)

    eye = jnp.eye(RG_BLOCKS, dtype=F32)
    bd = lambda t: jnp.einsum('ncd,nm->ncmd', t, eye).reshape(RG_WIDTH, RG_WIDTH)
    wg = jnp.concatenate([bd(p['rg_w_a'][l, 0]), bd(p['rg_w_i'][l, 0]),
                          bd(p['rg_w_a'][l, 1]), bd(p['rg_w_i'][l, 1])], axis=1).astype(BF16)
    bg = jnp.concatenate([p['rg_b_a'][l, 0], p['rg_b_i'][l, 0], p['rg_b_a'][l, 1], p['rg_b_i'][l, 1]])[None, :]

    pad_row = lambda v: jnp.concatenate([v.reshape(-1), jnp.zeros((LANE - N_STREAMS,), F32)])[None, :]
    bc_col = lambda v: jnp.broadcast_to(v.reshape(-1, 1), (N_STREAMS, LANE))

    nck = D_FF // FFN_CK
    pair = lambda t: jnp.concatenate([t[..., :D_FF].reshape(t.shape[:-1] + (nck, FFN_CK)),
                                      t[..., D_FF:].reshape(t.shape[:-1] + (nck, FFN_CK))], axis=-1)
    wu = jnp.moveaxis(pair(p['ffn_w_up'][l]), 1, 0).astype(BF16)
    fcw = jnp.moveaxis(pair(p['ffn_conv_w'][l]), 1, 0)
    fcb = pair(p['ffn_conv_b'][l])[:, None, :]
    wd = p['ffn_w_down'][l].astype(BF16)

    return dict(
        ln1_g=p['ln1_g'][l][None, :], w1=w1, wabt=wabt,
        gq=p['mla_q_norm_g'][l][None, :], wq=wq_ext, gkv=p['mla_kv_norm_g'][l][None, :],
        wk=wk_ext, wv=wv_ext,
        rg_cw=p['rg_conv_w'][l], rg_cb=p['rg_conv_b'][l][None, :], wg=wg, bg=bg, lam=p['rg_lam'][l],
        gdn_cw=p['gdn_conv_w'][l], alr=pad_row(p['gdn_a_log'][l]), dtr=pad_row(p['gdn_dt_bias'][l]),
        alc=bc_col(p['gdn_a_log'][l]), dtc=bc_col(p['gdn_dt_bias'][l]), ng=p['gdn_norm_g'][l][None, :],
        wb=p['w_branch'][l].astype(BF16), wo=p['w_out'][l].astype(BF16),
        ln2_g=p['ln2_g'][l][None, :], wu=wu, fcw=fcw, fcb=fcb, wd=wd)


def _rope_tables(S):
    inv = 1.0 / (ROPE_THETA ** (jnp.arange(0, MLA_ROPE, 2, dtype=F32) / MLA_ROPE))
    ang = jnp.arange(S, dtype=F32)[:, None] * inv[None, :]
    cos, sin = jnp.cos(ang), jnp.sin(ang)
    scale = MLA_QK ** -0.5
    csq = scale * jnp.concatenate([jnp.ones((S, MLA_NOPE), F32), cos, cos, sin, sin], axis=1)
    csk = jnp.concatenate([cos, cos, sin, sin, jnp.zeros((S, 64), F32)], axis=1)
    return csq, csk


def _trunk(x, mod, layers, final_g, csq, csk):
    B, S, D = x.shape
    x2d = x.reshape(B * S, D)
    tm = min(512, S)
    for l, w in enumerate(layers):
        m = mod[l].reshape(B, 1, 6 * D)
        mla_in, rg_in, gdn_in, ab, abt, gates = _in_projection(x2d, m, w['ln1_g'], w['w1'], w['wabt'], S, tm)
        o_mla = _mla(mla_in, csq, csk, w['gq'], w['wq'], w['gkv'], w['wk'], w['wv'], B, S)
        o_rg = _rglru(rg_in, w['rg_cw'], w['rg_cb'], w['wg'], w['bg'], w['lam'], B, S)
        o_gdn = _gdn(gdn_in, ab, abt, w['gdn_cw'], w['alr'], w['dtr'], w['alc'], w['dtc'], w['ng'], B, S)
        x2d = _merge(x2d, m, o_mla, o_rg, o_gdn, gates, w['wb'], w['wo'], S, tm)
        x2d = _ffn(x2d, m, w['ln2_g'], w['wu'], w['fcw'], w['fcb'], w['wd'], final_g, S, min(1024, S),
                   final=(l == len(layers) - 1))
    return x2d.reshape(B, S, D)


def kernel(x_prompt, x_sample, c_prompt, c_sample, ln1_g, w_mod, b_mod, w_in, mla_q_norm_g, mla_w_uq, mla_kv_norm_g, mla_w_ukv, rg_conv_w, rg_conv_b, rg_w_a, rg_b_a, rg_w_i, rg_b_i, rg_lam, gdn_conv_w, gdn_a_log, gdn_dt_bias, gdn_norm_g, w_branch, w_out, ln2_g, ffn_w_up, ffn_conv_w, ffn_conv_b, ffn_w_down, final_norm_g):
    p = dict(ln1_g=ln1_g, w_in=w_in, mla_q_norm_g=mla_q_norm_g, mla_w_uq=mla_w_uq,
             mla_kv_norm_g=mla_kv_norm_g, mla_w_ukv=mla_w_ukv, rg_conv_w=rg_conv_w, rg_conv_b=rg_conv_b,
             rg_w_a=rg_w_a, rg_b_a=rg_b_a, rg_w_i=rg_w_i, rg_b_i=rg_b_i, rg_lam=rg_lam,
             gdn_conv_w=gdn_conv_w, gdn_a_log=gdn_a_log, gdn_dt_bias=gdn_dt_bias, gdn_norm_g=gdn_norm_g,
             w_branch=w_branch, w_out=w_out, ln2_g=ln2_g, ffn_w_up=ffn_w_up, ffn_conv_w=ffn_conv_w,
             ffn_conv_b=ffn_conv_b, ffn_w_down=ffn_w_down)
    L = w_in.shape[0]
    layers = [_prep_layer(l, p) for l in range(L)]
    Bp = x_prompt.shape[0]
    mod = _modulation(jnp.concatenate([c_prompt, c_sample], axis=0), w_mod, b_mod)
    fg = final_norm_g[None, :]
    outs = []
    for x, m in ((x_prompt, mod[:, :Bp]), (x_sample, mod[:, Bp:])):
        csq, csk = _rope_tables(x.shape[1])
        outs.append(_trunk(x, m, layers, fg, csq, csk))
    return tuple(outs)
```

```python
import functools
import math

import jax
import jax.numpy as jnp
import numpy as np
from jax import lax
from jax.experimental import pallas as pl
from jax.experimental.pallas import tpu as pltpu

D_MODEL = 1024
DEPTH = 2
MLA_HEADS = 8
MLA_Q_LORA = 384
MLA_KV_LORA = 256
MLA_NOPE = 64
MLA_ROPE = 32
MLA_V = 64
MLA_QK = MLA_NOPE + MLA_ROPE
ROPE_THETA = 10000.0
RG_WIDTH = 512
RG_BLOCKS = 8
RG_C = 8.0
GDN_HEADS = 4
GDN_DK = 128
GDN_DV = 128
GDN_CHUNK = 64
D_FF = 2816
EPS = 1e-6

LANE = 128
HEAD_PAD = 128
VMEM_LIMIT = 56 * 1024 * 1024
FFN_CK = 256
N_STREAMS = 2 * GDN_HEADS
GDN_PAIRS_PER_STEP = 4

F32 = jnp.float32
BF16 = jnp.bfloat16


def _cparams(*sem):
    return pltpu.CompilerParams(dimension_semantics=sem, vmem_limit_bytes=VMEM_LIMIT)


def _const_spec(shape):
    nd = len(shape)
    return pl.BlockSpec(shape, lambda *_: (0,) * nd, pipeline_mode=pl.Buffered(1))


def _sigmoid(x):
    return 0.5 * jnp.tanh(0.5 * x) + 0.5


def _silu(x):
    h = 0.5 * x
    return h * jnp.tanh(h) + h


def _softplus(x):
    return jnp.maximum(x, 0.0) + jnp.log(1.0 + jnp.exp(-jnp.abs(x)))


def _gelu_tanh(x):
    c = math.sqrt(2.0 / math.pi)
    return 0.5 * x * (1.0 + jnp.tanh(c * (x + 0.044715 * (x * x * x))))


def _rms(x, g):
    return x * lax.rsqrt(jnp.mean(x * x, axis=-1, keepdims=True) + EPS) * g


def _dot(a, b):
    return jnp.dot(a, b, preferred_element_type=F32)


def _dot_nt(a, b):
    return lax.dot_general(a, b, (((1,), (1,)), ((), ())), preferred_element_type=F32)


def _dot_tn(a, b):
    return lax.dot_general(a, b, (((0,), (0,)), ((), ())), preferred_element_type=F32)


def _split3(x):
    x1 = x.astype(BF16)
    r1 = x - x1.astype(F32)
    x2 = r1.astype(BF16)
    x3 = (r1 - x2.astype(F32)).astype(BF16)
    return x1, x2, x3


def _mod_kernel(c_ref, w_ref, b_ref, o_ref):
    c = c_ref[...]
    a1, a2, a3 = _split3(_silu(c))
    w1, w2, w3 = _split3(w_ref[...])
    acc = _dot(a1, w1) + (_dot(a1, w2) + _dot(a2, w1)) + (_dot(a1, w3) + _dot(a2, w2) + _dot(a3, w1))
    o_ref[...] = acc + b_ref[...]


def _modulation(c, w_mod, b_mod):
    L, D, N = w_mod.shape
    Bt = c.shape[0]
    tn = 1024
    return pl.pallas_call(
        _mod_kernel,
        grid=(L, N // tn),
        in_specs=[pl.BlockSpec((Bt, D), lambda l, j: (0, 0)),
                  pl.BlockSpec((None, D, tn), lambda l, j: (l, 0, j)),
                  pl.BlockSpec((None, 1, tn), lambda l, j: (l, 0, j))],
        out_specs=pl.BlockSpec((None, Bt, tn), lambda l, j: (l, 0, j)),
        out_shape=jax.ShapeDtypeStruct((L, Bt, N), F32),
        compiler_params=_cparams("arbitrary", "arbitrary"),
        name="modulation",
    )(c, w_mod, b_mod.reshape(L, 1, N))


W_MLA, W_RG, W_GDN, W_AB, W_GATES = 768, 1024, 2048, 128, 3 * D_MODEL
_IN_OFFS = np.cumsum([0, W_MLA, W_RG, W_GDN, W_AB, W_GATES])


def _inproj_kernel(x_ref, mod_ref, g_ref, w_ref,
                   mla_ref, rg_ref, gdn_ref, ab_ref, gates_ref):
    D = D_MODEL
    x = x_ref[...]
    sh = mod_ref[:, 0:D]
    sc = mod_ref[:, D:2 * D]
    h = (_rms(x, g_ref[...]) * (1.0 + sc) + sh).astype(BF16)
    o = _IN_OFFS
    mla_ref[...] = _dot(h, w_ref[:, o[0]:o[1]]).astype(BF16)
    rg_ref[...] = _dot(h, w_ref[:, o[1]:o[2]]).astype(BF16)
    gdn_ref[...] = _dot(h, w_ref[:, o[2]:o[3]]).astype(BF16)
    ab_ref[...] = _dot(h, w_ref[:, o[3]:o[4]])
    gates_ref[...] = _dot(h, w_ref[:, o[4]:o[5]]).astype(BF16)


def _in_projection(x2d, mod, ln_g, w1, S, tm):
    T, D = x2d.shape
    per_seq = S // tm
    row = lambda i: (i, 0)
    outs = [(W_MLA, BF16), (W_RG, BF16), (W_GDN, BF16), (W_AB, F32), (W_GATES, BF16)]
    return pl.pallas_call(
        _inproj_kernel,
        grid=(T // tm,),
        in_specs=[pl.BlockSpec((tm, D), row),
                  pl.BlockSpec((None, 1, 6 * D), lambda i: (i // per_seq, 0, 0)),
                  _const_spec((1, D)),
                  _const_spec(w1.shape)],
        out_specs=[pl.BlockSpec((tm, w), row) for w, _ in outs],
        out_shape=[jax.ShapeDtypeStruct((T, w), dt) for w, dt in outs],
        compiler_params=_cparams("arbitrary"),
        name="in_projection",
    )(x2d, mod, ln_g, w1)


def _mla_kernel(in_ref, csq_ref, csk_ref, gq_ref, wq_ref, gkv_ref, wk_ref, wv_ref,
                o_ref, q_s, k_s, v_s, *, S, tq, tr):
    hp = pl.program_id(1)

    @pl.when(hp == 0)
    def _project():
        for r in range(S // tr):
            rows = slice(r * tr, (r + 1) * tr)
            qd = in_ref[rows, 0:MLA_Q_LORA].astype(F32)
            ckv = in_ref[rows, MLA_Q_LORA:MLA_Q_LORA + MLA_KV_LORA].astype(F32)
            kr = in_ref[rows, 640:768].astype(F32) * csk_ref[rows, :]
            cq = _rms(qd, gq_ref[...]).astype(BF16)
            ckvn = _rms(ckv, gkv_ref[...]).astype(BF16)
            q = _dot(cq, wq_ref[...])
            kin = jnp.concatenate([ckvn, kr.astype(BF16)], axis=1)
            k = _dot(kin, wk_ref[...])
            v = _dot(ckvn, wv_ref[...])
            cs = csq_ref[rows, :]
            for p in range(MLA_HEADS // 2):
                for hh in range(2):
                    h = 2 * p + hh
                    q_s[p, rows, hh * LANE:(hh + 1) * LANE] = (
                        q[:, h * LANE:(h + 1) * LANE] * cs).astype(BF16)
                k_s[p, rows, :] = k[:, 2 * p * LANE:(2 * p + 2) * LANE].astype(BF16)
                v_s[p, rows, :] = v[:, 2 * p * LANE:(2 * p + 2) * LANE].astype(BF16)

    for qt in range(S // tq):
        rows = slice(qt * tq, (qt + 1) * tq)
        acc = None
        for hh in range(2):
            lanes = slice(hh * LANE, (hh + 1) * LANE)
            q = q_s[hp, rows, lanes]
            k = k_s[hp, :, lanes]
            s = _dot_nt(q, k)
            m = jnp.max(s, axis=-1, keepdims=True)
            p = jnp.exp(s - m)
            l = jnp.sum(p, axis=-1, keepdims=True)
            o = _dot(p.astype(BF16), v_s[hp, :, lanes]) * (1.0 / l)
            acc = o if acc is None else acc + o
        o_ref[rows, :] = acc.astype(BF16)


def _mla(mla_in, csq, csk, gq, wq, gkv, wk, wv, B, S):
    tq = min(512, S)
    tr = min(512, S)
    npair = MLA_HEADS // 2
    kern = functools.partial(_mla_kernel, S=S, tq=tq, tr=tr)
    return pl.pallas_call(
        kern,
        grid=(B, npair),
        in_specs=[pl.BlockSpec((S, W_MLA), lambda b, p: (b, 0)),
                  _const_spec(csq.shape), _const_spec(csk.shape),
                  _const_spec(gq.shape), _const_spec(wq.shape),
                  _const_spec(gkv.shape), _const_spec(wk.shape), _const_spec(wv.shape)],
        out_specs=pl.BlockSpec((S, LANE), lambda b, p: (b, p)),
        out_shape=jax.ShapeDtypeStruct((B * S, MLA_HEADS * MLA_V), BF16),
        scratch_shapes=[pltpu.VMEM((npair, S, 2 * LANE), BF16),
                        pltpu.VMEM((npair, S, 2 * LANE), BF16),
                        pltpu.VMEM((npair, S, 2 * LANE), BF16)],
        compiler_params=_cparams("arbitrary", "arbitrary"),
        name="mla_attention",
    )(mla_in, csq, csk, gq, wq, gkv, wk, wv)


def _conv4_window(xw, w_ref, R):
    acc = xw[7:7 + R] * w_ref[0:1, :]
    acc = acc + xw[8:8 + R] * w_ref[1:2, :]
    acc = acc + xw[9:9 + R] * w_ref[2:3, :]
    acc = acc + xw[10:10 + R] * w_ref[3:4, :]
    return acc


def _scan_rows(a, b, carry, reverse):
    R, W = a.shape
    a = a.reshape(R // 8, 8, W)
    b = b.reshape(R // 8, 8, W)
    sub = lax.broadcasted_iota(jnp.int32, a.shape, 1)
    for d in (1, 2, 4):
        shift, ok = (8 - d, sub < 8 - d) if reverse else (d, sub >= d)
        a_sh = pltpu.roll(a, shift, 1)
        b_sh = pltpu.roll(b, shift, 1)
        b = b + a * jnp.where(ok, b_sh, 0.0)
        a = a * jnp.where(ok, a_sh, 1.0)
    tiles = [None] * (R // 8)
    for j in (range(R // 8 - 1, -1, -1) if reverse else range(R // 8)):
        t = b[j] + a[j] * carry
        tiles[j] = t
        carry = t[0:1] if reverse else t[7:8]
    return jnp.concatenate(tiles, axis=0), carry


def _rglru_kernel(in_ref, cw_ref, cb_ref, wg_ref, bg_ref, lam_ref, o_ref,
                  xpad, hf_s, xc_s, carry_s, *, S, R):
    W = RG_WIDTH
    nchunk = S // R
    xpad[0:8, :] = jnp.zeros((8, W), F32)
    xpad[S + 8:S + 16, :] = jnp.zeros((8, W), F32)
    xpad[8:S + 8, :] = in_ref[:, 0:W].astype(F32)
    nsp = -RG_C * _softplus(-lam_ref[...])

    def direction(d, n):
        r0 = pl.multiple_of(n * R, R)
        if d == 0:
            xc = _conv4_window(xpad[pl.ds(r0, R + 16), :], cw_ref, R) + cb_ref[...]
            xc_s[pl.ds(r0, R), :] = xc
        else:
            xc = xc_s[pl.ds(r0, R), :]
        gates = _dot(xc.astype(BF16), wg_ref[:, 2 * d * W:(2 * d + 2) * W]) \
            + bg_ref[:, 2 * d * W:(2 * d + 2) * W]
        r = _sigmoid(gates[:, :W])
        i = _sigmoid(gates[:, W:])
        log_a = nsp[d:d + 1, :] * r
        a = jnp.exp(log_a)
        th = jnp.tanh(log_a)
        m2 = -2.0 * th / (1.0 - th)
        mult = jnp.where(m2 > 0.0, m2 * lax.rsqrt(m2), 0.0)
        t = r0 + lax.broadcasted_iota(jnp.int32, (R, W), 0)
        first = t == (S - 1 if d == 1 else 0)
        mult = jnp.where(first, 1.0, mult)
        h, carry = _scan_rows(a, mult * (i * xc), carry_s[...], reverse=(d == 1))
        carry_s[...] = carry
        return r0, h

    carry_s[...] = jnp.zeros((1, W), F32)

    def fwd(n, c):
        r0, h = direction(0, n)
        hf_s[pl.ds(r0, R), :] = h
        return c

    lax.fori_loop(0, nchunk, fwd, 0)
    carry_s[...] = jnp.zeros((1, W), F32)

    def bwd(n, c):
        r0, h = direction(1, nchunk - 1 - n)
        gate = in_ref[pl.ds(r0, R), W:2 * W].astype(F32)
        o_ref[pl.ds(r0, R), :] = ((hf_s[pl.ds(r0, R), :] + h) * _gelu_tanh(gate)).astype(BF16)
        return c

    lax.fori_loop(0, nchunk, bwd, 0)


def _rglru(rg_in, cw, cb, wg, bg, lam, B, S):
    R = min(256, S)
    kern = functools.partial(_rglru_kernel, S=S, R=R)
    return pl.pallas_call(
        kern,
        grid=(B,),
        in_specs=[pl.BlockSpec((S, W_RG), lambda b: (b, 0)),
                  _const_spec(cw.shape), _const_spec(cb.shape), _const_spec(wg.shape),
                  _const_spec(bg.shape), _const_spec(lam.shape)],
        out_specs=pl.BlockSpec((S, RG_WIDTH), lambda b: (b, 0)),
        out_shape=jax.ShapeDtypeStruct((B * S, RG_WIDTH), BF16),
        scratch_shapes=[pltpu.VMEM((S + 16, RG_WIDTH), F32),
                        pltpu.VMEM((S, RG_WIDTH), F32),
                        pltpu.VMEM((S, RG_WIDTH), F32),
                        pltpu.VMEM((1, RG_WIDTH), F32)],
        compiler_params=_cparams("arbitrary"),
        name="rglru",
    )(rg_in, cw, cb, wg, bg, lam)


def _tri_consts():
    C = GDN_CHUNK
    r = lax.broadcasted_iota(jnp.int32, (2 * C, 2 * C), 0)
    c = lax.broadcasted_iota(jnp.int32, (2 * C, 2 * C), 1)
    same = (r // C) == (c // C)
    one = lambda m: jnp.where(m, 1.0, 0.0).astype(BF16)
    return r, c, same, one


def _unit_tri_inverse(a_list, ri, ci, eye, width):
    blk = lambda n: (ri // n) == (ci // n)
    ident = jnp.where(eye, 1.0, 0.0)
    ps = [jnp.where(blk(8), -a, 0.0).astype(BF16) for a in a_list]
    ts = [ident + p.astype(F32) for p in ps]
    for _ in range(2):
        ps = [_dot(p, p).astype(BF16) for p in ps]
        ts = [t + _dot(t.astype(BF16), p) for t, p in zip(ts, ps)]
    n = 8
    while n < width:
        sel = blk(2 * n) & jnp.logical_not(blk(n))
        offs = [jnp.where(sel, a, 0.0).astype(BF16) for a in a_list]
        tbs = [t.astype(BF16) for t in ts]
        mids = [_dot(tb, off).astype(BF16) for tb, off in zip(tbs, offs)]
        ts = [t - _dot(mid, tb) for t, mid, tb in zip(ts, mids, tbs)]
        n *= 2
    return ts


def _gdn_kernel(in_ref, ab_ref, cw_ref, alr_ref, dtr_ref, ng_ref,
                o_ref, xpad, qn_s, kn_s, vn_s, gcol_s, rcol_s, bcol_s, grow_s, rrow_s, brow_s,
                m1_s, m2_s, st_s, o_s, *, S):
    C = GDN_CHUNK
    H = GDN_HEADS
    NC = S // C
    HW = H * GDN_DK
    R = min(256, S)

    xpad[0:8, :] = jnp.zeros((8, HW), F32)
    xpad[S + 8:S + 16, :] = jnp.zeros((8, HW), F32)
    for grp, dst in enumerate((qn_s, kn_s, vn_s)):
        xpad[8:S + 8, :] = in_ref[:, grp * HW:(grp + 1) * HW].astype(F32)

        def conv_body(n, c, grp=grp, dst=dst):
            r0 = pl.multiple_of(n * R, R)
            xw = xpad[pl.ds(r0, R + 16), :]
            y = _silu(_conv4_window(xw, cw_ref.at[:, grp * HW:(grp + 1) * HW], R))
            if grp < 2:
                for h in range(H):
                    t = y[:, h * LANE:(h + 1) * LANE]
                    t = t * lax.rsqrt(jnp.sum(t * t, axis=-1, keepdims=True) + EPS)
                    dst[pl.ds(r0, R), h * LANE:(h + 1) * LANE] = t.astype(BF16)
            else:
                dst[pl.ds(r0, R), :] = y.astype(BF16)
            return c

        lax.fori_loop(0, S // R, conv_body, 0)

    r, c, same, one = _tri_consts()
    lane = lax.broadcasted_iota(jnp.int32, (2 * C, LANE), 1)
    is_fwd_lane = lane < H
    lane_lo = lax.broadcasted_iota(jnp.int32, (N_STREAMS, LANE), 1) < C
    tl_incl = one(same & (c <= r))
    tu_incl = one(same & (c >= r))
    tl_excl = one(same & (c < r))
    tu_excl = one(same & (c > r))

    def sum3(parts, mat):
        acc = None
        for p_ in parts:
            t = _dot(mat, p_)
            acc = t if acc is None else acc + t
        return acc

    for pr in range(S // (2 * C)):
        rows = slice(pr * 2 * C, (pr + 1) * 2 * C)
        ab = ab_ref[rows, :]
        g_col = -jnp.exp(alr_ref[...]) * _softplus(ab + dtr_ref[...])
        parts = _split3(g_col)
        gcol_s[rows, :] = jnp.where(is_fwd_lane, sum3(parts, tl_incl), sum3(parts, tu_incl))
        rcol_s[rows, :] = jnp.where(is_fwd_lane, sum3(parts, tu_excl), sum3(parts, tl_excl))
        bcol_s[rows, :] = _sigmoid(ab)
        for src, dst, lo in ((gcol_s, grow_s, 0), (rcol_s, rrow_s, 0), (bcol_s, brow_s, N_STREAMS)):
            t = src[rows, :].T[lo:lo + N_STREAMS, :]
            sw = pltpu.roll(t, C, 1)
            dst[2 * pr] = jnp.where(lane_lo, t, sw)
            dst[2 * pr + 1] = jnp.where(lane_lo, sw, t)

    P = 2 * C
    ri = lax.broadcasted_iota(jnp.int32, (P, P), 0)
    ci = lax.broadcasted_iota(jnp.int32, (P, P), 1)
    eye = ri == ci
    top = ri < C
    same_dir = (ri // C) == (ci // C)
    bot = jnp.logical_not(top)
    incl = same_dir & ((top & (ci <= ri)) | (bot & (ci >= ri)))
    strict = same_dir & ((top & (ci < ri)) | (bot & (ci > ri)))
    lane_f = lax.broadcasted_iota(jnp.int32, (H, LANE), 1) < C
    scale = GDN_DK ** -0.5

    def pair_rows(ref, rf, rb, lanes):
        return jnp.concatenate([ref[pl.ds(rf, C), lanes], ref[pl.ds(rb, C), lanes]], axis=0)

    def pair_lanes(arr_s, n, nb):
        return jnp.where(lane_f, arr_s[n][0:H], arr_s[nb][H:2 * H])

    def col_pair(col_f, col_b, lane_idx_f, lane_idx_b):
        return jnp.concatenate([jnp.broadcast_to(col_f[:, lane_idx_f:lane_idx_f + 1], (C, P)),
                                jnp.broadcast_to(col_b[:, lane_idx_b:lane_idx_b + 1], (C, P))], axis=0)

    G = GDN_PAIRS_PER_STEP

    def phase1(i):
        a_l, decay_l, qk_l, brow_l = [], [], [], []
        for n in [G * i + g for g in range(G)]:
            nb = NC - 1 - n
            rf = pl.multiple_of(n * C, C)
            rb = pl.multiple_of(nb * C, C)
            gcol_f, gcol_b = gcol_s[pl.ds(rf, C), :], gcol_s[pl.ds(rb, C), :]
            bcol_f, bcol_b = bcol_s[pl.ds(rf, C), :], bcol_s[pl.ds(rb, C), :]
            grow = pair_lanes(grow_s, n, nb)
            brow = pair_lanes(brow_s, n, nb)
            for h in range(H):
                lanes = slice(h * LANE, (h + 1) * LANE)
                kp = pair_rows(kn_s, rf, rb, lanes)
                qp = pair_rows(qn_s, rf, rb, lanes)
                kq = _dot_nt(jnp.concatenate([kp, qp], axis=0), kp)
                gi = col_pair(gcol_f, gcol_b, h, H + h)
                bi = col_pair(bcol_f, bcol_b, N_STREAMS + h, N_STREAMS + H + h)
                decay = jnp.where(incl, jnp.exp(jnp.minimum(gi - grow[h:h + 1, :], 0.0)), 0.0)
                a_l.append(jnp.where(strict, bi * kq[0:P] * decay, 0.0))
                decay_l.append(decay)
                qk_l.append(kq[P:2 * P])
                brow_l.append(brow[h:h + 1, :])
        t_l = _unit_tri_inverse(a_l, ri, ci, eye, C)
        for j in range(G * H):
            m1_s[G * i * H + j] = (t_l[j] * brow_l[j]).astype(BF16)
            m2_s[G * i * H + j] = (qk_l[j] * decay_l[j] * scale).astype(BF16)

    st_s[...] = jnp.zeros(st_s.shape, F32)
    o_s[...] = jnp.zeros(o_s.shape, F32)

    def phase2(n):
        nb = NC - 1 - n
        rf = pl.multiple_of(n * C, C)
        rb = pl.multiple_of(nb * C, C)
        eg_f, eg_b = jnp.exp(gcol_s[pl.ds(rf, C), :]), jnp.exp(gcol_s[pl.ds(rb, C), :])
        ek_f, ek_b = jnp.exp(rcol_s[pl.ds(rf, C), :]), jnp.exp(rcol_s[pl.ds(rb, C), :])
        etot_f = jnp.exp(grow_s[n] + rrow_s[n])
        etot_b = jnp.exp(grow_s[nb] + rrow_s[nb])
        kp_l, kqs_l, vp_l = [], [], []
        for h in range(H):
            lanes = slice(h * LANE, (h + 1) * LANE)
            kp = pair_rows(kn_s, rf, rb, lanes)
            qp = pair_rows(qn_s, rf, rb, lanes)
            kp_l.append(kp)
            vp_l.append(pair_rows(vn_s, rf, rb, lanes))
            kqs_l.append(_dot(jnp.concatenate([kp, qp], axis=0), st_s[h].astype(BF16)))
        vnew_l, oq_l = [], []
        for h in range(H):
            kqs = kqs_l[h]
            ks = jnp.concatenate([kqs[0:C, 0:LANE], kqs[C:P, LANE:2 * LANE]], axis=0)
            qs = jnp.concatenate([kqs[P:P + C, 0:LANE], kqs[P + C:2 * P, LANE:2 * LANE]], axis=0)
            egc = col_pair(eg_f, eg_b, h, H + h)
            oq_l.append(egc * scale * qs)
            r1 = (vp_l[h].astype(F32) - egc * ks).astype(BF16)
            vnew_l.append(_dot(m1_s[n * H + h], r1))
        for h in range(H):
            lanes = slice(h * LANE, (h + 1) * LANE)
            v_new = vnew_l[h]
            o = _dot(m2_s[n * H + h], v_new.astype(BF16)) + oq_l[h]
            x = col_pair(ek_f, ek_b, h, H + h) * v_new
            xbd = jnp.concatenate([jnp.where(top, x, 0.0), jnp.where(top, 0.0, x)], axis=1).astype(BF16)
            etot = jnp.concatenate([etot_f[h:h + 1, :], etot_b[H + h:H + h + 1, :]], axis=1)
            st_s[h] = st_s[h] * etot + _dot_tn(kp_l[h], xbd)
            o_s[pl.ds(rf, C), lanes] += o[0:C]
            o_s[pl.ds(rb, C), lanes] += o[C:P]

    nstep = NC // G
    phase1(0)

    def fused(i, carry):
        for g in range(G):
            phase2(G * i + g)
        phase1(jnp.minimum(i + 1, nstep - 1))
        return carry

    lax.fori_loop(0, nstep, fused, 0)

    def phase3(n, carry):
        r0 = pl.multiple_of(n * R, R)
        z = in_ref[pl.ds(r0, R), 3 * HW:4 * HW].astype(F32)
        for h in range(H):
            lanes = slice(h * LANE, (h + 1) * LANE)
            o = o_s[pl.ds(r0, R), lanes]
            o = _rms(o, ng_ref[...]) * _silu(z[:, lanes])
            o_ref[pl.ds(r0, R), lanes] = o.astype(BF16)
        return carry

    lax.fori_loop(0, S // R, phase3, 0)


def _gdn(gdn_in, ab, cw, alr, dtr, ng, B, S):
    C = GDN_CHUNK
    NC = S // C
    HW = GDN_HEADS * GDN_DK
    kern = functools.partial(_gdn_kernel, S=S)
    scratch = [pltpu.VMEM((S + 16, HW), F32),
               pltpu.VMEM((S, HW), BF16), pltpu.VMEM((S, HW), BF16), pltpu.VMEM((S, HW), BF16),
               pltpu.VMEM((S, LANE), F32), pltpu.VMEM((S, LANE), F32), pltpu.VMEM((S, LANE), F32),
               pltpu.VMEM((NC, N_STREAMS, LANE), F32), pltpu.VMEM((NC, N_STREAMS, LANE), F32),
               pltpu.VMEM((NC, N_STREAMS, LANE), F32),
               pltpu.VMEM((NC * GDN_HEADS, 2 * C, 2 * C), BF16),
               pltpu.VMEM((NC * GDN_HEADS, 2 * C, 2 * C), BF16),
               pltpu.VMEM((GDN_HEADS, GDN_DK, 2 * GDN_DV), F32),
               pltpu.VMEM((S, HW), F32)]
    return pl.pallas_call(
        kern,
        grid=(B,),
        in_specs=[pl.BlockSpec((S, W_GDN), lambda b: (b, 0)),
                  pl.BlockSpec((S, W_AB), lambda b: (b, 0)),
                  _const_spec(cw.shape), _const_spec(alr.shape), _const_spec(dtr.shape),
                  _const_spec(ng.shape)],
        out_specs=pl.BlockSpec((S, HW), lambda b: (b, 0)),
        out_shape=jax.ShapeDtypeStruct((B * S, HW), BF16),
        scratch_shapes=scratch,
        compiler_params=_cparams("arbitrary"),
        name="gated_deltanet",
    )(gdn_in, ab, cw, alr, dtr, ng)


def _merge_kernel(x_ref, mod_ref, om_ref, or_ref, og_ref, gates_ref, wb_ref, wo_ref, o_ref):
    D = D_MODEL
    mixed = None
    for n, src in enumerate((om_ref, or_ref, og_ref)):
        g = _sigmoid(gates_ref[:, n * D:(n + 1) * D].astype(F32))
        term = g * _dot(src[...], wb_ref[n])
        mixed = term if mixed is None else mixed + term
    gt1 = mod_ref[:, 2 * D:3 * D]
    o_ref[...] = x_ref[...] + gt1 * _dot(mixed.astype(BF16), wo_ref[...])


def _merge(x2d, mod, o_mla, o_rg, o_gdn, gates, wb, wo, S, tm):
    T, D = x2d.shape
    per_seq = S // tm
    row = lambda i: (i, 0)
    return pl.pallas_call(
        _merge_kernel,
        grid=(T // tm,),
        in_specs=[pl.BlockSpec((tm, D), row),
                  pl.BlockSpec((None, 1, 6 * D), lambda i: (i // per_seq, 0, 0)),
                  pl.BlockSpec((tm, 512), row), pl.BlockSpec((tm, 512), row), pl.BlockSpec((tm, 512), row),
                  pl.BlockSpec((tm, W_GATES), row),
                  _const_spec(wb.shape), _const_spec(wo.shape)],
        out_specs=pl.BlockSpec((tm, D), row),
        out_shape=jax.ShapeDtypeStruct((T, D), F32),
        compiler_params=_cparams("arbitrary"),
        name="branch_merge",
    )(x2d, mod, o_mla, o_rg, o_gdn, gates, wb, wo)


def _ffn_kernel(x_ref, xp_ref, xn_ref, mod_ref, g_ref, wu_ref, cw_ref, cb_ref, wd_ref, fg_ref,
                o_ref, act_s, *, tm, per_seq, final):
    D = D_MODEL
    i = pl.program_id(0)
    sh = mod_ref[:, 3 * D:4 * D]
    sc = mod_ref[:, 4 * D:5 * D]
    gt2 = mod_ref[:, 5 * D:6 * D]
    x = x_ref[...]
    h = (_rms(x, g_ref[...]) * (1.0 + sc) + sh).astype(BF16)
    xh = jnp.concatenate([xp_ref[...], xn_ref[...]], axis=0)
    hh = (_rms(xh, g_ref[...]) * (1.0 + sc) + sh).astype(BF16)
    has_prev = jnp.where(i % per_seq == 0, 0.0, 1.0)
    has_next = jnp.where(i % per_seq == per_seq - 1, 0.0, 1.0)
    row8 = lax.broadcasted_iota(jnp.int32, (8, 2 * FFN_CK), 0)
    nck = D_FF // FFN_CK
    split = (nck + 1) // 2
    acc = None
    for j in range(nck):
        w = wu_ref[j]
        up = _dot(h, w)
        uh = _dot(hh, w)
        prev = uh[7:8, :] * has_prev
        nxt = uh[8:9, :] * has_next
        dn = pltpu.roll(up, 1, 0)
        dn = jnp.concatenate([jnp.where(row8 == 0, prev, dn[0:8]), dn[8:]], axis=0)
        un = pltpu.roll(up, tm - 1, 0)
        un = jnp.concatenate([un[:tm - 8], jnp.where(row8 == 7, nxt, un[tm - 8:])], axis=0)
        cw = cw_ref[j]
        y = dn * cw[0:1, :] + up * cw[1:2, :] + un * cw[2:3, :] + cb_ref[j]
        act = _silu(y[:, FFN_CK:]) * y[:, :FFN_CK]
        act_s[:, j * FFN_CK:(j + 1) * FFN_CK] = act.astype(BF16)
        if j == split - 1:
            acc = _dot(act_s[:, 0:split * FFN_CK], wd_ref[0:split * FFN_CK, :])
    acc = acc + _dot(act_s[:, split * FFN_CK:], wd_ref[split * FFN_CK:, :])
    y = x + gt2 * acc
    if final:
        y = _rms(y, fg_ref[...])
    o_ref[...] = y


def _ffn(x2d, mod, ln_g, wu, cw, cb, wd, fg, S, tm, final):
    T, D = x2d.shape
    per_seq = S // tm
    hb = tm // 8
    nblk8 = T // 8
    kern = functools.partial(_ffn_kernel, tm=tm, per_seq=per_seq, final=final)
    return pl.pallas_call(
        kern,
        grid=(T // tm,),
        in_specs=[pl.BlockSpec((tm, D), lambda i: (i, 0)),
                  pl.BlockSpec((8, D), lambda i: (jnp.maximum(i * hb - 1, 0), 0)),
                  pl.BlockSpec((8, D), lambda i: (jnp.minimum((i + 1) * hb, nblk8 - 1), 0)),
                  pl.BlockSpec((None, 1, 6 * D), lambda i: (i // per_seq, 0, 0)),
                  _const_spec((1, D)),
                  _const_spec(wu.shape), _const_spec(cw.shape), _const_spec(cb.shape),
                  _const_spec(wd.shape), _const_spec((1, D))],
        out_specs=pl.BlockSpec((tm, D), lambda i: (i, 0)),
        out_shape=jax.ShapeDtypeStruct((T, D), F32),
        scratch_shapes=[pltpu.VMEM((tm, D_FF), BF16)],
        compiler_params=_cparams("arbitrary"),
        name="conv_glu_ffn",
    )(x2d, x2d, x2d, mod, ln_g, wu, cw, cb, wd, fg)


def _rot_cols(w):
    half = w.shape[-1] // 2
    return jnp.concatenate([-w[..., half:], w[..., :half]], axis=-1)


def _prep_layer(l, p):
    D = D_MODEL
    w = p['w_in'][l]
    o = np.cumsum([0, MLA_Q_LORA, MLA_KV_LORA + MLA_ROPE, RG_WIDTH, RG_WIDTH, 3 * 512, 512, 8, 8, 3 * D])
    kro = o[1] + MLA_KV_LORA
    w_kr = w[:, kro:o[2]]
    w1 = jnp.concatenate([
        w[:, o[0]:o[1]], w[:, o[1]:kro], w_kr, _rot_cols(w_kr), jnp.zeros((D, 64), F32),
        w[:, o[2]:o[4]],
        w[:, o[4]:o[6]],
        w[:, o[6]:o[8]], jnp.zeros((D, W_AB - 16), F32),
        w[:, o[8]:o[9]]], axis=1).astype(BF16)

    H = MLA_HEADS
    wq = p['mla_w_uq'][l].reshape(MLA_Q_LORA, H, MLA_QK)
    rope = wq[..., MLA_NOPE:]
    wq_ext = jnp.concatenate([wq[..., :MLA_NOPE], rope, _rot_cols(rope)], axis=-1)
    wq_ext = wq_ext.reshape(MLA_Q_LORA, H * HEAD_PAD).astype(BF16)
    wkv = p['mla_w_ukv'][l].reshape(MLA_KV_LORA, H, MLA_NOPE + MLA_V)
    wk_top = jnp.concatenate([wkv[..., :MLA_NOPE], jnp.zeros((MLA_KV_LORA, H, 64), F32)], axis=-1)
    e = np.zeros((128, H, HEAD_PAD), np.float32)
    for j in range(MLA_ROPE):
        for src in (j, MLA_ROPE + j):
            e[src, :, MLA_NOPE + j] = 1.0
            e[src, :, MLA_NOPE + MLA_ROPE + j] = 1.0
    wk_ext = jnp.concatenate([wk_top, jnp.asarray(e)], axis=0).reshape(MLA_KV_LORA + 128, H * HEAD_PAD).astype(BF16)
    wv = wkv[..., MLA_NOPE:]
    z = jnp.zeros_like(wv)
    even = (np.arange(H) % 2 == 0)[None, :, None]
    wv_ext = jnp.concatenate([jnp.where(even, wv, z), jnp.where(even, z, wv)], axis=-1)
    wv_ext = wv_ext.reshape(MLA_KV_LORA, H * HEAD_PAD).astype(BF16)

    eye = jnp.eye(RG_BLOCKS, dtype=F32)
    bd = lambda t: jnp.einsum('ncd,nm->ncmd', t, eye).reshape(RG_WIDTH, RG_WIDTH)
    wg = jnp.concatenate([bd(p['rg_w_a'][l, 0]), bd(p['rg_w_i'][l, 0]),
                          bd(p['rg_w_a'][l, 1]), bd(p['rg_w_i'][l, 1])], axis=1).astype(BF16)
    bg = jnp.concatenate([p['rg_b_a'][l, 0], p['rg_b_i'][l, 0], p['rg_b_a'][l, 1], p['rg_b_i'][l, 1]])[None, :]

    pad_row = lambda v: jnp.concatenate([v.reshape(-1), jnp.zeros((LANE - N_STREAMS,), F32)])[None, :]

    nck = D_FF // FFN_CK
    pair = lambda t: jnp.concatenate([t[..., :D_FF].reshape(t.shape[:-1] + (nck, FFN_CK)),
                                      t[..., D_FF:].reshape(t.shape[:-1] + (nck, FFN_CK))], axis=-1)
    wu = jnp.moveaxis(pair(p['ffn_w_up'][l]), 1, 0).astype(BF16)
    fcw = jnp.moveaxis(pair(p['ffn_conv_w'][l]), 1, 0)
    fcb = pair(p['ffn_conv_b'][l])[:, None, :]
    wd = p['ffn_w_down'][l].astype(BF16)

    return dict(
        ln1_g=p['ln1_g'][l][None, :], w1=w1,
        gq=p['mla_q_norm_g'][l][None, :], wq=wq_ext, gkv=p['mla_kv_norm_g'][l][None, :],
        wk=wk_ext, wv=wv_ext,
        rg_cw=p['rg_conv_w'][l], rg_cb=p['rg_conv_b'][l][None, :], wg=wg, bg=bg, lam=p['rg_lam'][l],
        gdn_cw=p['gdn_conv_w'][l], alr=pad_row(p['gdn_a_log'][l]), dtr=pad_row(p['gdn_dt_bias'][l]),
        ng=p['gdn_norm_g'][l][None, :],
        wb=p['w_branch'][l].astype(BF16), wo=p['w_out'][l].astype(BF16),
        ln2_g=p['ln2_g'][l][None, :], wu=wu, fcw=fcw, fcb=fcb, wd=wd)


def _rope_tables(S):
    inv = 1.0 / (ROPE_THETA ** (jnp.arange(0, MLA_ROPE, 2, dtype=F32) / MLA_ROPE))
    ang = jnp.arange(S, dtype=F32)[:, None] * inv[None, :]
    cos, sin = jnp.cos(ang), jnp.sin(ang)
    scale = MLA_QK ** -0.5
    csq = scale * jnp.concatenate([jnp.ones((S, MLA_NOPE), F32), cos, cos, sin, sin], axis=1)
    csk = jnp.concatenate([cos, cos, sin, sin, jnp.zeros((S, 64), F32)], axis=1)
    return csq, csk


def _trunk(x, mod, layers, final_g, csq, csk):
    B, S, D = x.shape
    x2d = x.reshape(B * S, D)
    tm = min(512, S)
    for l, w in enumerate(layers):
        m = mod[l].reshape(B, 1, 6 * D)
        mla_in, rg_in, gdn_in, ab, gates = _in_projection(x2d, m, w['ln1_g'], w['w1'], S, tm)
        o_mla = _mla(mla_in, csq, csk, w['gq'], w['wq'], w['gkv'], w['wk'], w['wv'], B, S)
        o_rg = _rglru(rg_in, w['rg_cw'], w['rg_cb'], w['wg'], w['bg'], w['lam'], B, S)
        o_gdn = _gdn(gdn_in, ab, w['gdn_cw'], w['alr'], w['dtr'], w['ng'], B, S)
        x2d = _merge(x2d, m, o_mla, o_rg, o_gdn, gates, w['wb'], w['wo'], S, tm)
        x2d = _ffn(x2d, m, w['ln2_g'], w['wu'], w['fcw'], w['fcb'], w['wd'], final_g, S, min(1024, S),
                   final=(l == len(layers) - 1))
    return x2d.reshape(B, S, D)


def kernel(x_prompt, x_sample, c_prompt, c_sample, ln1_g, w_mod, b_mod, w_in, mla_q_norm_g, mla_w_uq, mla_kv_norm_g, mla_w_ukv, rg_conv_w, rg_conv_b, rg_w_a, rg_b_a, rg_w_i, rg_b_i, rg_lam, gdn_conv_w, gdn_a_log, gdn_dt_bias, gdn_norm_g, w_branch, w_out, ln2_g, ffn_w_up, ffn_conv_w, ffn_conv_b, ffn_w_down, final_norm_g):
    p = dict(ln1_g=ln1_g, w_in=w_in, mla_q_norm_g=mla_q_norm_g, mla_w_uq=mla_w_uq,
             mla_kv_norm_g=mla_kv_norm_g, mla_w_ukv=mla_w_ukv, rg_conv_w=rg_conv_w, rg_conv_b=rg_conv_b,
             rg_w_a=rg_w_a, rg_b_a=rg_b_a, rg_w_i=rg_w_i, rg_b_i=rg_b_i, rg_lam=rg_lam,
             gdn_conv_w=gdn_conv_w, gdn_a_log=gdn_a_log, gdn_dt_bias=gdn_dt_bias, gdn_norm_g=gdn_norm_g,
             w_branch=w_branch, w_out=w_out, ln2_g=ln2_g, ffn_w_up=ffn_w_up, ffn_conv_w=ffn_conv_w,
             ffn_conv_b=ffn_conv_b, ffn_w_down=ffn_w_down)
    L = w_in.shape[0]
    layers = [_prep_layer(l, p) for l in range(L)]
    Bp = x_prompt.shape[0]
    mod = _modulation(jnp.concatenate([c_prompt, c_sample], axis=0), w_mod, b_mod)
    fg = final_norm_g[None, :]
    outs = []
    for x, m in ((x_prompt, mod[:, :Bp]), (x_sample, mod[:, Bp:])):
        csq, csk = _rope_tables(x.shape[1])
        outs.append(_trunk(x, m, layers, fg, csq, csk))
    return tuple(outs)
```

```python
import functools
import math

import jax
import jax.numpy as jnp
import numpy as np
from jax import lax
from jax.experimental import pallas as pl
from jax.experimental.pallas import tpu as pltpu

D_MODEL = 1024
DEPTH = 2
MLA_HEADS = 8
MLA_Q_LORA = 384
MLA_KV_LORA = 256
MLA_NOPE = 64
MLA_ROPE = 32
MLA_V = 64
MLA_QK = MLA_NOPE + MLA_ROPE
ROPE_THETA = 10000.0
RG_WIDTH = 512
RG_BLOCKS = 8
RG_C = 8.0
GDN_HEADS = 4
GDN_DK = 128
GDN_DV = 128
GDN_CHUNK = 64
D_FF = 2816
EPS = 1e-6

LANE = 128
HEAD_PAD = 128
VMEM_LIMIT = 56 * 1024 * 1024
FFN_CK = 256
N_STREAMS = 2 * GDN_HEADS
GDN_PAIRS_PER_STEP = 4

F32 = jnp.float32
BF16 = jnp.bfloat16


def _cparams(*sem):
    return pltpu.CompilerParams(dimension_semantics=sem, vmem_limit_bytes=VMEM_LIMIT)


def _const_spec(shape):
    nd = len(shape)
    return pl.BlockSpec(shape, lambda *_: (0,) * nd, pipeline_mode=pl.Buffered(1))


def _sigmoid(x):
    return 0.5 * jnp.tanh(0.5 * x) + 0.5


def _silu(x):
    h = 0.5 * x
    return h * jnp.tanh(h) + h


def _softplus(x):
    return jnp.maximum(x, 0.0) + jnp.log(1.0 + jnp.exp(-jnp.abs(x)))


def _gelu_tanh(x):
    c = math.sqrt(2.0 / math.pi)
    return 0.5 * x * (1.0 + jnp.tanh(c * (x + 0.044715 * (x * x * x))))


def _rms(x, g):
    return x * lax.rsqrt(jnp.mean(x * x, axis=-1, keepdims=True) + EPS) * g


def _dot(a, b):
    return jnp.dot(a, b, preferred_element_type=F32)


def _dot_nt(a, b):
    return lax.dot_general(a, b, (((1,), (1,)), ((), ())), preferred_element_type=F32)


def _dot_tn(a, b):
    return lax.dot_general(a, b, (((0,), (0,)), ((), ())), preferred_element_type=F32)


def _split3(x):
    x1 = x.astype(BF16)
    r1 = x - x1.astype(F32)
    x2 = r1.astype(BF16)
    x3 = (r1 - x2.astype(F32)).astype(BF16)
    return x1, x2, x3


def _mod_kernel(c_ref, w_ref, b_ref, o_ref):
    c = c_ref[...]
    a1, a2, a3 = _split3(_silu(c))
    w1, w2, w3 = _split3(w_ref[...])
    acc = _dot(a1, w1) + (_dot(a1, w2) + _dot(a2, w1)) + (_dot(a1, w3) + _dot(a2, w2) + _dot(a3, w1))
    o_ref[...] = acc + b_ref[...]


def _modulation(c, w_mod, b_mod):
    L, D, N = w_mod.shape
    Bt = c.shape[0]
    tn = 1024
    return pl.pallas_call(
        _mod_kernel,
        grid=(L, N // tn),
        in_specs=[pl.BlockSpec((Bt, D), lambda l, j: (0, 0)),
                  pl.BlockSpec((None, D, tn), lambda l, j: (l, 0, j)),
                  pl.BlockSpec((None, 1, tn), lambda l, j: (l, 0, j))],
        out_specs=pl.BlockSpec((None, Bt, tn), lambda l, j: (l, 0, j)),
        out_shape=jax.ShapeDtypeStruct((L, Bt, N), F32),
        compiler_params=_cparams("arbitrary", "arbitrary"),
        name="modulation",
    )(c, w_mod, b_mod.reshape(L, 1, N))


W_MLA, W_RG, W_GDN, W_AB, W_GATES = 768, 1024, 2048, 128, 3 * D_MODEL
_IN_OFFS = np.cumsum([0, W_MLA, W_RG, W_GDN, W_AB, W_GATES])


def _inproj_kernel(x_ref, mod_ref, g_ref, w_ref,
                   mla_ref, rg_ref, gdn_ref, ab_ref, gates_ref):
    D = D_MODEL
    x = x_ref[...]
    sh = mod_ref[:, 0:D]
    sc = mod_ref[:, D:2 * D]
    h = (_rms(x, g_ref[...]) * (1.0 + sc) + sh).astype(BF16)
    o = _IN_OFFS
    mla_ref[...] = _dot(h, w_ref[:, o[0]:o[1]]).astype(BF16)
    rg_ref[...] = _dot(h, w_ref[:, o[1]:o[2]]).astype(BF16)
    gdn_ref[...] = _dot(h, w_ref[:, o[2]:o[3]]).astype(BF16)
    ab_ref[...] = _dot(h, w_ref[:, o[3]:o[4]])
    gates_ref[...] = _dot(h, w_ref[:, o[4]:o[5]]).astype(BF16)


def _in_projection(x2d, mod, ln_g, w1, S, tm):
    T, D = x2d.shape
    per_seq = S // tm
    row = lambda i: (i, 0)
    outs = [(W_MLA, BF16), (W_RG, BF16), (W_GDN, BF16), (W_AB, F32), (W_GATES, BF16)]
    return pl.pallas_call(
        _inproj_kernel,
        grid=(T // tm,),
        in_specs=[pl.BlockSpec((tm, D), row),
                  pl.BlockSpec((None, 1, 6 * D), lambda i: (i // per_seq, 0, 0)),
                  _const_spec((1, D)),
                  _const_spec(w1.shape)],
        out_specs=[pl.BlockSpec((tm, w), row) for w, _ in outs],
        out_shape=[jax.ShapeDtypeStruct((T, w), dt) for w, dt in outs],
        compiler_params=_cparams("arbitrary"),
        name="in_projection",
    )(x2d, mod, ln_g, w1)


def _mla_kernel(in_ref, csq_ref, csk_ref, gq_ref, wq_ref, gkv_ref, wk_ref, wv_ref,
                o_ref, q_s, k_s, v_s, *, S, tq, tr):
    hp = pl.program_id(1)

    @pl.when(hp == 0)
    def _project():
        for r in range(S // tr):
            rows = slice(r * tr, (r + 1) * tr)
            qd = in_ref[rows, 0:MLA_Q_LORA].astype(F32)
            ckv = in_ref[rows, MLA_Q_LORA:MLA_Q_LORA + MLA_KV_LORA].astype(F32)
            kr = in_ref[rows, 640:768].astype(F32) * csk_ref[rows, :]
            cq = _rms(qd, gq_ref[...]).astype(BF16)
            ckvn = _rms(ckv, gkv_ref[...]).astype(BF16)
            q = _dot(cq, wq_ref[...])
            kin = jnp.concatenate([ckvn, kr.astype(BF16)], axis=1)
            k = _dot(kin, wk_ref[...])
            v = _dot(ckvn, wv_ref[...])
            cs = csq_ref[rows, :]
            for p in range(MLA_HEADS // 2):
                for hh in range(2):
                    h = 2 * p + hh
                    q_s[p, rows, hh * LANE:(hh + 1) * LANE] = (
                        q[:, h * LANE:(h + 1) * LANE] * cs).astype(BF16)
                k_s[p, rows, :] = k[:, 2 * p * LANE:(2 * p + 2) * LANE].astype(BF16)
                v_s[p, rows, :] = v[:, 2 * p * LANE:(2 * p + 2) * LANE].astype(BF16)

    for qt in range(S // tq):
        rows = slice(qt * tq, (qt + 1) * tq)
        acc = None
        for hh in range(2):
            lanes = slice(hh * LANE, (hh + 1) * LANE)
            q = q_s[hp, rows, lanes]
            k = k_s[hp, :, lanes]
            s = _dot_nt(q, k)
            m = jnp.max(s, axis=-1, keepdims=True)
            p = jnp.exp(s - m)
            l = jnp.sum(p, axis=-1, keepdims=True)
            o = _dot(p.astype(BF16), v_s[hp, :, lanes]) * (1.0 / l)
            acc = o if acc is None else acc + o
        o_ref[rows, :] = acc.astype(BF16)


def _mla(mla_in, csq, csk, gq, wq, gkv, wk, wv, B, S):
    tq = min(512, S)
    tr = min(512, S)
    npair = MLA_HEADS // 2
    kern = functools.partial(_mla_kernel, S=S, tq=tq, tr=tr)
    return pl.pallas_call(
        kern,
        grid=(B, npair),
        in_specs=[pl.BlockSpec((S, W_MLA), lambda b, p: (b, 0)),
                  _const_spec(csq.shape), _const_spec(csk.shape),
                  _const_spec(gq.shape), _const_spec(wq.shape),
                  _const_spec(gkv.shape), _const_spec(wk.shape), _const_spec(wv.shape)],
        out_specs=pl.BlockSpec((S, LANE), lambda b, p: (b, p)),
        out_shape=jax.ShapeDtypeStruct((B * S, MLA_HEADS * MLA_V), BF16),
        scratch_shapes=[pltpu.VMEM((npair, S, 2 * LANE), BF16),
                        pltpu.VMEM((npair, S, 2 * LANE), BF16),
                        pltpu.VMEM((npair, S, 2 * LANE), BF16)],
        compiler_params=_cparams("arbitrary", "arbitrary"),
        name="mla_attention",
    )(mla_in, csq, csk, gq, wq, gkv, wk, wv)


def _conv4_window(xw, w_ref, R):
    acc = xw[7:7 + R] * w_ref[0:1, :]
    acc = acc + xw[8:8 + R] * w_ref[1:2, :]
    acc = acc + xw[9:9 + R] * w_ref[2:3, :]
    acc = acc + xw[10:10 + R] * w_ref[3:4, :]
    return acc


def _scan_rows(a, b, carry, reverse):
    R, W = a.shape
    a = a.reshape(R // 8, 8, W)
    b = b.reshape(R // 8, 8, W)
    sub = lax.broadcasted_iota(jnp.int32, a.shape, 1)
    for d in (1, 2, 4):
        shift, ok = (8 - d, sub < 8 - d) if reverse else (d, sub >= d)
        a_sh = pltpu.roll(a, shift, 1)
        b_sh = pltpu.roll(b, shift, 1)
        b = b + a * jnp.where(ok, b_sh, 0.0)
        a = a * jnp.where(ok, a_sh, 1.0)
    tiles = [None] * (R // 8)
    for j in (range(R // 8 - 1, -1, -1) if reverse else range(R // 8)):
        t = b[j] + a[j] * carry
        tiles[j] = t
        carry = t[0:1] if reverse else t[7:8]
    return jnp.concatenate(tiles, axis=0), carry


def _rglru_kernel(in_ref, cw_ref, cb_ref, wg_ref, bg_ref, lam_ref, o_ref,
                  xpad, hf_s, xc_s, carry_s, *, S, R):
    W = RG_WIDTH
    nchunk = S // R
    xpad[0:8, :] = jnp.zeros((8, W), F32)
    xpad[S + 8:S + 16, :] = jnp.zeros((8, W), F32)
    xpad[8:S + 8, :] = in_ref[:, 0:W].astype(F32)
    nsp = -RG_C * _softplus(-lam_ref[...])

    def direction(d, n):
        r0 = pl.multiple_of(n * R, R)
        if d == 0:
            xc = _conv4_window(xpad[pl.ds(r0, R + 16), :], cw_ref, R) + cb_ref[...]
            xc_s[pl.ds(r0, R), :] = xc
        else:
            xc = xc_s[pl.ds(r0, R), :]
        gates = _dot(xc.astype(BF16), wg_ref[:, 2 * d * W:(2 * d + 2) * W]) \
            + bg_ref[:, 2 * d * W:(2 * d + 2) * W]
        r = _sigmoid(gates[:, :W])
        i = _sigmoid(gates[:, W:])
        log_a = nsp[d:d + 1, :] * r
        a = jnp.exp(log_a)
        th = jnp.tanh(log_a)
        m2 = -2.0 * th / (1.0 - th)
        mult = jnp.where(m2 > 0.0, m2 * lax.rsqrt(m2), 0.0)
        t = r0 + lax.broadcasted_iota(jnp.int32, (R, W), 0)
        first = t == (S - 1 if d == 1 else 0)
        mult = jnp.where(first, 1.0, mult)
        h, carry = _scan_rows(a, mult * (i * xc), carry_s[...], reverse=(d == 1))
        carry_s[...] = carry
        return r0, h

    carry_s[...] = jnp.zeros((1, W), F32)

    def fwd(n, c):
        r0, h = direction(0, n)
        hf_s[pl.ds(r0, R), :] = h
        return c

    lax.fori_loop(0, nchunk, fwd, 0)
    carry_s[...] = jnp.zeros((1, W), F32)

    def bwd(n, c):
        r0, h = direction(1, nchunk - 1 - n)
        gate = in_ref[pl.ds(r0, R), W:2 * W].astype(F32)
        o_ref[pl.ds(r0, R), :] = ((hf_s[pl.ds(r0, R), :] + h) * _gelu_tanh(gate)).astype(BF16)
        return c

    lax.fori_loop(0, nchunk, bwd, 0)


def _rglru(rg_in, cw, cb, wg, bg, lam, B, S):
    R = min(256, S)
    kern = functools.partial(_rglru_kernel, S=S, R=R)
    return pl.pallas_call(
        kern,
        grid=(B,),
        in_specs=[pl.BlockSpec((S, W_RG), lambda b: (b, 0)),
                  _const_spec(cw.shape), _const_spec(cb.shape), _const_spec(wg.shape),
                  _const_spec(bg.shape), _const_spec(lam.shape)],
        out_specs=pl.BlockSpec((S, RG_WIDTH), lambda b: (b, 0)),
        out_shape=jax.ShapeDtypeStruct((B * S, RG_WIDTH), BF16),
        scratch_shapes=[pltpu.VMEM((S + 16, RG_WIDTH), F32),
                        pltpu.VMEM((S, RG_WIDTH), F32),
                        pltpu.VMEM((S, RG_WIDTH), F32),
                        pltpu.VMEM((1, RG_WIDTH), F32)],
        compiler_params=_cparams("arbitrary"),
        name="rglru",
    )(rg_in, cw, cb, wg, bg, lam)


def _tri_consts():
    C = GDN_CHUNK
    r = lax.broadcasted_iota(jnp.int32, (2 * C, 2 * C), 0)
    c = lax.broadcasted_iota(jnp.int32, (2 * C, 2 * C), 1)
    same = (r // C) == (c // C)
    one = lambda m: jnp.where(m, 1.0, 0.0).astype(BF16)
    return r, c, same, one


def _unit_tri_inverse(a_list, ri, ci, eye, width):
    blk = lambda n: (ri // n) == (ci // n)
    ident = jnp.where(eye, 1.0, 0.0)
    ps = [jnp.where(blk(8), -a, 0.0).astype(BF16) for a in a_list]
    ts = [ident + p.astype(F32) for p in ps]
    for _ in range(2):
        ps = [_dot(p, p).astype(BF16) for p in ps]
        ts = [t + _dot(t.astype(BF16), p) for t, p in zip(ts, ps)]
    n = 8
    while n < width:
        sel = blk(2 * n) & jnp.logical_not(blk(n))
        offs = [jnp.where(sel, a, 0.0).astype(BF16) for a in a_list]
        tbs = [t.astype(BF16) for t in ts]
        mids = [_dot(tb, off).astype(BF16) for tb, off in zip(tbs, offs)]
        ts = [t - _dot(mid, tb) for t, mid, tb in zip(ts, mids, tbs)]
        n *= 2
    return ts


def _gdn_kernel(in_ref, ab_ref, cw_ref, alr_ref, dtr_ref, ng_ref,
                o_ref, xpad, qn_s, kn_s, vn_s, gcol_s, rcol_s, bcol_s, grow_s, rrow_s, brow_s,
                m1_s, m2_s, st_s, o_s, *, S):
    C = GDN_CHUNK
    H = GDN_HEADS
    NC = S // C
    HW = H * GDN_DK
    R = min(256, S)

    xpad[0:8, :] = jnp.zeros((8, HW), F32)
    xpad[S + 8:S + 16, :] = jnp.zeros((8, HW), F32)
    for grp, dst in enumerate((qn_s, kn_s, vn_s)):
        xpad[8:S + 8, :] = in_ref[:, grp * HW:(grp + 1) * HW].astype(F32)

        def conv_body(n, c, grp=grp, dst=dst):
            r0 = pl.multiple_of(n * R, R)
            xw = xpad[pl.ds(r0, R + 16), :]
            y = _silu(_conv4_window(xw, cw_ref.at[:, grp * HW:(grp + 1) * HW], R))
            if grp < 2:
                for h in range(H):
                    t = y[:, h * LANE:(h + 1) * LANE]
                    t = t * lax.rsqrt(jnp.sum(t * t, axis=-1, keepdims=True) + EPS)
                    dst[pl.ds(r0, R), h * LANE:(h + 1) * LANE] = t.astype(BF16)
            else:
                dst[pl.ds(r0, R), :] = y.astype(BF16)
            return c

        lax.fori_loop(0, S // R, conv_body, 0)

    r, c, same, one = _tri_consts()
    lane = lax.broadcasted_iota(jnp.int32, (2 * C, LANE), 1)
    is_fwd_lane = lane < H
    lane_lo = lax.broadcasted_iota(jnp.int32, (N_STREAMS, LANE), 1) < C
    tl_incl = one(same & (c <= r))
    tu_incl = one(same & (c >= r))
    tl_excl = one(same & (c < r))
    tu_excl = one(same & (c > r))

    def sum3(parts, mat):
        acc = None
        for p_ in parts:
            t = _dot(mat, p_)
            acc = t if acc is None else acc + t
        return acc

    for pr in range(S // (2 * C)):
        rows = slice(pr * 2 * C, (pr + 1) * 2 * C)
        ab = ab_ref[rows, :]
        g_col = -jnp.exp(alr_ref[...]) * _softplus(ab + dtr_ref[...])
        parts = _split3(g_col)
        gcol_s[rows, :] = jnp.where(is_fwd_lane, sum3(parts, tl_incl), sum3(parts, tu_incl))
        rcol_s[rows, :] = jnp.where(is_fwd_lane, sum3(parts, tu_excl), sum3(parts, tl_excl))
        bcol_s[rows, :] = _sigmoid(ab)
        for src, dst, lo in ((gcol_s, grow_s, 0), (rcol_s, rrow_s, 0), (bcol_s, brow_s, N_STREAMS)):
            t = src[rows, :].T[lo:lo + N_STREAMS, :]
            sw = pltpu.roll(t, C, 1)
            dst[2 * pr] = jnp.where(lane_lo, t, sw)
            dst[2 * pr + 1] = jnp.where(lane_lo, sw, t)

    P = 2 * C
    ri = lax.broadcasted_iota(jnp.int32, (P, P), 0)
    ci = lax.broadcasted_iota(jnp.int32, (P, P), 1)
    eye = ri == ci
    top = ri < C
    same_dir = (ri // C) == (ci // C)
    bot = jnp.logical_not(top)
    incl = same_dir & ((top & (ci <= ri)) | (bot & (ci >= ri)))
    strict = same_dir & ((top & (ci < ri)) | (bot & (ci > ri)))
    lane_f = lax.broadcasted_iota(jnp.int32, (H, LANE), 1) < C
    scale = GDN_DK ** -0.5

    def pair_rows(ref, rf, rb, lanes):
        return jnp.concatenate([ref[pl.ds(rf, C), lanes], ref[pl.ds(rb, C), lanes]], axis=0)

    def pair_lanes(arr_s, n, nb):
        return jnp.where(lane_f, arr_s[n][0:H], arr_s[nb][H:2 * H])

    def col_pair(col_f, col_b, lane_idx_f, lane_idx_b):
        return jnp.concatenate([jnp.broadcast_to(col_f[:, lane_idx_f:lane_idx_f + 1], (C, P)),
                                jnp.broadcast_to(col_b[:, lane_idx_b:lane_idx_b + 1], (C, P))], axis=0)

    G = GDN_PAIRS_PER_STEP

    def phase1(i):
        a_l, decay_l, qk_l, brow_l = [], [], [], []
        for n in [G * i + g for g in range(G)]:
            nb = NC - 1 - n
            rf = pl.multiple_of(n * C, C)
            rb = pl.multiple_of(nb * C, C)
            gcol_f, gcol_b = gcol_s[pl.ds(rf, C), :], gcol_s[pl.ds(rb, C), :]
            bcol_f, bcol_b = bcol_s[pl.ds(rf, C), :], bcol_s[pl.ds(rb, C), :]
            grow = pair_lanes(grow_s, n, nb)
            brow = pair_lanes(brow_s, n, nb)
            for h in range(H):
                lanes = slice(h * LANE, (h + 1) * LANE)
                kp = pair_rows(kn_s, rf, rb, lanes)
                qp = pair_rows(qn_s, rf, rb, lanes)
                kq = _dot_nt(jnp.concatenate([kp, qp], axis=0), kp)
                gi = col_pair(gcol_f, gcol_b, h, H + h)
                bi = col_pair(bcol_f, bcol_b, N_STREAMS + h, N_STREAMS + H + h)
                decay = jnp.where(incl, jnp.exp(jnp.minimum(gi - grow[h:h + 1, :], 0.0)), 0.0)
                a_l.append(jnp.where(strict, bi * kq[0:P] * decay, 0.0))
                decay_l.append(decay)
                qk_l.append(kq[P:2 * P])
                brow_l.append(brow[h:h + 1, :])
        t_l = _unit_tri_inverse(a_l, ri, ci, eye, C)
        for j in range(G * H):
            m1_s[G * i * H + j] = (t_l[j] * brow_l[j]).astype(BF16)
            m2_s[G * i * H + j] = (qk_l[j] * decay_l[j] * scale).astype(BF16)

    st_s[...] = jnp.zeros(st_s.shape, F32)
    o_s[...] = jnp.zeros(o_s.shape, F32)

    def phase2(n):
        nb = NC - 1 - n
        rf = pl.multiple_of(n * C, C)
        rb = pl.multiple_of(nb * C, C)
        eg_f, eg_b = jnp.exp(gcol_s[pl.ds(rf, C), :]), jnp.exp(gcol_s[pl.ds(rb, C), :])
        ek_f, ek_b = jnp.exp(rcol_s[pl.ds(rf, C), :]), jnp.exp(rcol_s[pl.ds(rb, C), :])
        etot_f = jnp.exp(grow_s[n] + rrow_s[n])
        etot_b = jnp.exp(grow_s[nb] + rrow_s[nb])
        kp_l, kqs_l, vp_l = [], [], []
        for h in range(H):
            lanes = slice(h * LANE, (h + 1) * LANE)
            kp = pair_rows(kn_s, rf, rb, lanes)
            qp = pair_rows(qn_s, rf, rb, lanes)
            kp_l.append(kp)
            vp_l.append(pair_rows(vn_s, rf, rb, lanes))
            kqs_l.append(_dot(jnp.concatenate([kp, qp], axis=0), st_s[h].astype(BF16)))
        vnew_l, oq_l = [], []
        for h in range(H):
            kqs = kqs_l[h]
            ks = jnp.concatenate([kqs[0:C, 0:LANE], kqs[C:P, LANE:2 * LANE]], axis=0)
            qs = jnp.concatenate([kqs[P:P + C, 0:LANE], kqs[P + C:2 * P, LANE:2 * LANE]], axis=0)
            egc = col_pair(eg_f, eg_b, h, H + h)
            oq_l.append(egc * scale * qs)
            r1 = (vp_l[h].astype(F32) - egc * ks).astype(BF16)
            vnew_l.append(_dot(m1_s[n * H + h], r1))
        for h in range(H):
            lanes = slice(h * LANE, (h + 1) * LANE)
            v_new = vnew_l[h]
            o = _dot(m2_s[n * H + h], v_new.astype(BF16)) + oq_l[h]
            x = col_pair(ek_f, ek_b, h, H + h) * v_new
            xbd = jnp.concatenate([jnp.where(top, x, 0.0), jnp.where(top, 0.0, x)], axis=1).astype(BF16)
            etot = jnp.concatenate([etot_f[h:h + 1, :], etot_b[H + h:H + h + 1, :]], axis=1)
            st_s[h] = st_s[h] * etot + _dot_tn(kp_l[h], xbd)
            o_s[pl.ds(rf, C), lanes] += o[0:C]
            o_s[pl.ds(rb, C), lanes] += o[C:P]

    nstep = NC // G
    phase1(0)

    def fused(i, carry):
        for g in range(G):
            phase2(G * i + g)
        phase1(jnp.minimum(i + 1, nstep - 1))
        return carry

    lax.fori_loop(0, nstep, fused, 0)

    def phase3(n, carry):
        r0 = pl.multiple_of(n * R, R)
        z = in_ref[pl.ds(r0, R), 3 * HW:4 * HW].astype(F32)
        for h in range(H):
            lanes = slice(h * LANE, (h + 1) * LANE)
            o = o_s[pl.ds(r0, R), lanes]
            o = _rms(o, ng_ref[...]) * _silu(z[:, lanes])
            o_ref[pl.ds(r0, R), lanes] = o.astype(BF16)
        return carry

    lax.fori_loop(0, S // R, phase3, 0)


def _gdn(gdn_in, ab, cw, alr, dtr, ng, B, S):
    C = GDN_CHUNK
    NC = S // C
    HW = GDN_HEADS * GDN_DK
    kern = functools.partial(_gdn_kernel, S=S)
    scratch = [pltpu.VMEM((S + 16, HW), F32),
               pltpu.VMEM((S, HW), BF16), pltpu.VMEM((S, HW), BF16), pltpu.VMEM((S, HW), BF16),
               pltpu.VMEM((S, LANE), F32), pltpu.VMEM((S, LANE), F32), pltpu.VMEM((S, LANE), F32),
               pltpu.VMEM((NC, N_STREAMS, LANE), F32), pltpu.VMEM((NC, N_STREAMS, LANE), F32),
               pltpu.VMEM((NC, N_STREAMS, LANE), F32),
               pltpu.VMEM((NC * GDN_HEADS, 2 * C, 2 * C), BF16),
               pltpu.VMEM((NC * GDN_HEADS, 2 * C, 2 * C), BF16),
               pltpu.VMEM((GDN_HEADS, GDN_DK, 2 * GDN_DV), F32),
               pltpu.VMEM((S, HW), F32)]
    return pl.pallas_call(
        kern,
        grid=(B,),
        in_specs=[pl.BlockSpec((S, W_GDN), lambda b: (b, 0)),
                  pl.BlockSpec((S, W_AB), lambda b: (b, 0)),
                  _const_spec(cw.shape), _const_spec(alr.shape), _const_spec(dtr.shape),
                  _const_spec(ng.shape)],
        out_specs=pl.BlockSpec((S, HW), lambda b: (b, 0)),
        out_shape=jax.ShapeDtypeStruct((B * S, HW), BF16),
        scratch_shapes=scratch,
        compiler_params=_cparams("arbitrary"),
        name="gated_deltanet",
    )(gdn_in, ab, cw, alr, dtr, ng)


def _merge_kernel(x_ref, mod_ref, om_ref, or_ref, og_ref, gates_ref, wb_ref, wo_ref, o_ref):
    D = D_MODEL
    mixed = None
    for n, src in enumerate((om_ref, or_ref, og_ref)):
        g = _sigmoid(gates_ref[:, n * D:(n + 1) * D].astype(F32))
        term = g * _dot(src[...], wb_ref[n])
        mixed = term if mixed is None else mixed + term
    gt1 = mod_ref[:, 2 * D:3 * D]
    o_ref[...] = x_ref[...] + gt1 * _dot(mixed.astype(BF16), wo_ref[...])


def _merge(x2d, mod, o_mla, o_rg, o_gdn, gates, wb, wo, S, tm):
    T, D = x2d.shape
    per_seq = S // tm
    row = lambda i: (i, 0)
    return pl.pallas_call(
        _merge_kernel,
        grid=(T // tm,),
        in_specs=[pl.BlockSpec((tm, D), row),
                  pl.BlockSpec((None, 1, 6 * D), lambda i: (i // per_seq, 0, 0)),
                  pl.BlockSpec((tm, 512), row), pl.BlockSpec((tm, 512), row), pl.BlockSpec((tm, 512), row),
                  pl.BlockSpec((tm, W_GATES), row),
                  _const_spec(wb.shape), _const_spec(wo.shape)],
        out_specs=pl.BlockSpec((tm, D), row),
        out_shape=jax.ShapeDtypeStruct((T, D), F32),
        compiler_params=_cparams("arbitrary"),
        name="branch_merge",
    )(x2d, mod, o_mla, o_rg, o_gdn, gates, wb, wo)


def _ffn_kernel(x_ref, xp_ref, xn_ref, mod_ref, g_ref, wu_ref, cw_ref, cb_ref, wd_ref, fg_ref,
                o_ref, act_s, *, tm, per_seq, final):
    D = D_MODEL
    i = pl.program_id(0)
    sh = mod_ref[:, 3 * D:4 * D]
    sc = mod_ref[:, 4 * D:5 * D]
    gt2 = mod_ref[:, 5 * D:6 * D]
    x = x_ref[...]
    h = (_rms(x, g_ref[...]) * (1.0 + sc) + sh).astype(BF16)
    xh = jnp.concatenate([xp_ref[...], xn_ref[...]], axis=0)
    hh = (_rms(xh, g_ref[...]) * (1.0 + sc) + sh).astype(BF16)
    has_prev = jnp.where(i % per_seq == 0, 0.0, 1.0)
    has_next = jnp.where(i % per_seq == per_seq - 1, 0.0, 1.0)
    row8 = lax.broadcasted_iota(jnp.int32, (8, 2 * FFN_CK), 0)
    nck = D_FF // FFN_CK
    split = (nck + 1) // 2
    acc = None
    hcat = jnp.concatenate([h, hh], axis=0)
    for j in range(nck):
        up_all = _dot(hcat, wu_ref[j])
        up = up_all[:tm]
        prev = up_all[tm + 7:tm + 8, :] * has_prev
        nxt = up_all[tm + 8:tm + 9, :] * has_next
        dn = pltpu.roll(up, 1, 0)
        dn = jnp.concatenate([jnp.where(row8 == 0, prev, dn[0:8]), dn[8:]], axis=0)
        un = pltpu.roll(up, tm - 1, 0)
        un = jnp.concatenate([un[:tm - 8], jnp.where(row8 == 7, nxt, un[tm - 8:])], axis=0)
        cw = cw_ref[j]
        y = dn * cw[0:1, :] + up * cw[1:2, :] + un * cw[2:3, :] + cb_ref[j]
        act = _silu(y[:, FFN_CK:]) * y[:, :FFN_CK]
        act_s[:, j * FFN_CK:(j + 1) * FFN_CK] = act.astype(BF16)
        if j == split - 1:
            acc = _dot(act_s[:, 0:split * FFN_CK], wd_ref[0:split * FFN_CK, :])
    acc = acc + _dot(act_s[:, split * FFN_CK:], wd_ref[split * FFN_CK:, :])
    y = x + gt2 * acc
    if final:
        y = _rms(y, fg_ref[...])
    o_ref[...] = y


def _ffn(x2d, mod, ln_g, wu, cw, cb, wd, fg, S, tm, final):
    T, D = x2d.shape
    per_seq = S // tm
    hb = tm // 8
    nblk8 = T // 8
    kern = functools.partial(_ffn_kernel, tm=tm, per_seq=per_seq, final=final)
    return pl.pallas_call(
        kern,
        grid=(T // tm,),
        in_specs=[pl.BlockSpec((tm, D), lambda i: (i, 0)),
                  pl.BlockSpec((8, D), lambda i: (jnp.maximum(i * hb - 1, 0), 0)),
                  pl.BlockSpec((8, D), lambda i: (jnp.minimum((i + 1) * hb, nblk8 - 1), 0)),
                  pl.BlockSpec((None, 1, 6 * D), lambda i: (i // per_seq, 0, 0)),
                  _const_spec((1, D)),
                  _const_spec(wu.shape), _const_spec(cw.shape), _const_spec(cb.shape),
                  _const_spec(wd.shape), _const_spec((1, D))],
        out_specs=pl.BlockSpec((tm, D), lambda i: (i, 0)),
        out_shape=jax.ShapeDtypeStruct((T, D), F32),
        scratch_shapes=[pltpu.VMEM((tm, D_FF), BF16)],
        compiler_params=_cparams("arbitrary"),
        name="conv_glu_ffn",
    )(x2d, x2d, x2d, mod, ln_g, wu, cw, cb, wd, fg)


def _rot_cols(w):
    half = w.shape[-1] // 2
    return jnp.concatenate([-w[..., half:], w[..., :half]], axis=-1)


def _prep_layer(l, p):
    D = D_MODEL
    w = p['w_in'][l]
    o = np.cumsum([0, MLA_Q_LORA, MLA_KV_LORA + MLA_ROPE, RG_WIDTH, RG_WIDTH, 3 * 512, 512, 8, 8, 3 * D])
    kro = o[1] + MLA_KV_LORA
    w_kr = w[:, kro:o[2]]
    w1 = jnp.concatenate([
        w[:, o[0]:o[1]], w[:, o[1]:kro], w_kr, _rot_cols(w_kr), jnp.zeros((D, 64), F32),
        w[:, o[2]:o[4]],
        w[:, o[4]:o[6]],
        w[:, o[6]:o[8]], jnp.zeros((D, W_AB - 16), F32),
        w[:, o[8]:o[9]]], axis=1).astype(BF16)

    H = MLA_HEADS
    wq = p['mla_w_uq'][l].reshape(MLA_Q_LORA, H, MLA_QK)
    rope = wq[..., MLA_NOPE:]
    wq_ext = jnp.concatenate([wq[..., :MLA_NOPE], rope, _rot_cols(rope)], axis=-1)
    wq_ext = wq_ext.reshape(MLA_Q_LORA, H * HEAD_PAD).astype(BF16)
    wkv = p['mla_w_ukv'][l].reshape(MLA_KV_LORA, H, MLA_NOPE + MLA_V)
    wk_top = jnp.concatenate([wkv[..., :MLA_NOPE], jnp.zeros((MLA_KV_LORA, H, 64), F32)], axis=-1)
    e = np.zeros((128, H, HEAD_PAD), np.float32)
    for j in range(MLA_ROPE):
        for src in (j, MLA_ROPE + j):
            e[src, :, MLA_NOPE + j] = 1.0
            e[src, :, MLA_NOPE + MLA_ROPE + j] = 1.0
    wk_ext = jnp.concatenate([wk_top, jnp.asarray(e)], axis=0).reshape(MLA_KV_LORA + 128, H * HEAD_PAD).astype(BF16)
    wv = wkv[..., MLA_NOPE:]
    z = jnp.zeros_like(wv)
    even = (np.arange(H) % 2 == 0)[None, :, None]
    wv_ext = jnp.concatenate([jnp.where(even, wv, z), jnp.where(even, z, wv)], axis=-1)
    wv_ext = wv_ext.reshape(MLA_KV_LORA, H * HEAD_PAD).astype(BF16)

    eye = jnp.eye(RG_BLOCKS, dtype=F32)
    bd = lambda t: jnp.einsum('ncd,nm->ncmd', t, eye).reshape(RG_WIDTH, RG_WIDTH)
    wg = jnp.concatenate([bd(p['rg_w_a'][l, 0]), bd(p['rg_w_i'][l, 0]),
                          bd(p['rg_w_a'][l, 1]), bd(p['rg_w_i'][l, 1])], axis=1).astype(BF16)
    bg = jnp.concatenate([p['rg_b_a'][l, 0], p['rg_b_i'][l, 0], p['rg_b_a'][l, 1], p['rg_b_i'][l, 1]])[None, :]

    pad_row = lambda v: jnp.concatenate([v.reshape(-1), jnp.zeros((LANE - N_STREAMS,), F32)])[None, :]

    nck = D_FF // FFN_CK
    pair = lambda t: jnp.concatenate([t[..., :D_FF].reshape(t.shape[:-1] + (nck, FFN_CK)),
                                      t[..., D_FF:].reshape(t.shape[:-1] + (nck, FFN_CK))], axis=-1)
    wu = jnp.moveaxis(pair(p['ffn_w_up'][l]), 1, 0).astype(BF16)
    fcw = jnp.moveaxis(pair(p['ffn_conv_w'][l]), 1, 0)
    fcb = pair(p['ffn_conv_b'][l])[:, None, :]
    wd = p['ffn_w_down'][l].astype(BF16)

    return dict(
        ln1_g=p['ln1_g'][l][None, :], w1=w1,
        gq=p['mla_q_norm_g'][l][None, :], wq=wq_ext, gkv=p['mla_kv_norm_g'][l][None, :],
        wk=wk_ext, wv=wv_ext,
        rg_cw=p['rg_conv_w'][l], rg_cb=p['rg_conv_b'][l][None, :], wg=wg, bg=bg, lam=p['rg_lam'][l],
        gdn_cw=p['gdn_conv_w'][l], alr=pad_row(p['gdn_a_log'][l]), dtr=pad_row(p['gdn_dt_bias'][l]),
        ng=p['gdn_norm_g'][l][None, :],
        wb=p['w_branch'][l].astype(BF16), wo=p['w_out'][l].astype(BF16),
        ln2_g=p['ln2_g'][l][None, :], wu=wu, fcw=fcw, fcb=fcb, wd=wd)


def _rope_tables(S):
    inv = 1.0 / (ROPE_THETA ** (jnp.arange(0, MLA_ROPE, 2, dtype=F32) / MLA_ROPE))
    ang = jnp.arange(S, dtype=F32)[:, None] * inv[None, :]
    cos, sin = jnp.cos(ang), jnp.sin(ang)
    scale = MLA_QK ** -0.5
    csq = scale * jnp.concatenate([jnp.ones((S, MLA_NOPE), F32), cos, cos, sin, sin], axis=1)
    csk = jnp.concatenate([cos, cos, sin, sin, jnp.zeros((S, 64), F32)], axis=1)
    return csq, csk


def _trunk(x, mod, layers, final_g, csq, csk):
    B, S, D = x.shape
    x2d = x.reshape(B * S, D)
    tm = min(512, S)
    for l, w in enumerate(layers):
        m = mod[l].reshape(B, 1, 6 * D)
        mla_in, rg_in, gdn_in, ab, gates = _in_projection(x2d, m, w['ln1_g'], w['w1'], S, tm)
        o_mla = _mla(mla_in, csq, csk, w['gq'], w['wq'], w['gkv'], w['wk'], w['wv'], B, S)
        o_rg = _rglru(rg_in, w['rg_cw'], w['rg_cb'], w['wg'], w['bg'], w['lam'], B, S)
        o_gdn = _gdn(gdn_in, ab, w['gdn_cw'], w['alr'], w['dtr'], w['ng'], B, S)
        x2d = _merge(x2d, m, o_mla, o_rg, o_gdn, gates, w['wb'], w['wo'], S, tm)
        x2d = _ffn(x2d, m, w['ln2_g'], w['wu'], w['fcw'], w['fcb'], w['wd'], final_g, S, min(1024, S),
                   final=(l == len(layers) - 1))
    return x2d.reshape(B, S, D)


def kernel(x_prompt, x_sample, c_prompt, c_sample, ln1_g, w_mod, b_mod, w_in, mla_q_norm_g, mla_w_uq, mla_kv_norm_g, mla_w_ukv, rg_conv_w, rg_conv_b, rg_w_a, rg_b_a, rg_w_i, rg_b_i, rg_lam, gdn_conv_w, gdn_a_log, gdn_dt_bias, gdn_norm_g, w_branch, w_out, ln2_g, ffn_w_up, ffn_conv_w, ffn_conv_b, ffn_w_down, final_norm_g):
    p = dict(ln1_g=ln1_g, w_in=w_in, mla_q_norm_g=mla_q_norm_g, mla_w_uq=mla_w_uq,
             mla_kv_norm_g=mla_kv_norm_g, mla_w_ukv=mla_w_ukv, rg_conv_w=rg_conv_w, rg_conv_b=rg_conv_b,
             rg_w_a=rg_w_a, rg_b_a=rg_b_a, rg_w_i=rg_w_i, rg_b_i=rg_b_i, rg_lam=rg_lam,
             gdn_conv_w=gdn_conv_w, gdn_a_log=gdn_a_log, gdn_dt_bias=gdn_dt_bias, gdn_norm_g=gdn_norm_g,
             w_branch=w_branch, w_out=w_out, ln2_g=ln2_g, ffn_w_up=ffn_w_up, ffn_conv_w=ffn_conv_w,
             ffn_conv_b=ffn_conv_b, ffn_w_down=ffn_w_down)
    L = w_in.shape[0]
    layers = [_prep_layer(l, p) for l in range(L)]
    Bp = x_prompt.shape[0]
    mod = _modulation(jnp.concatenate([c_prompt, c_sample], axis=0), w_mod, b_mod)
    fg = final_norm_g[None, :]
    outs = []
    for x, m in ((x_prompt, mod[:, :Bp]), (x_sample, mod[:, Bp:])):
        csq, csk = _rope_tables(x.shape[1])
        outs.append(_trunk(x, m, layers, fg, csq, csk))
    return tuple(outs)
```

```python
import functools
import math

import jax
import jax.numpy as jnp
import numpy as np
from jax import lax
from jax.experimental import pallas as pl
from jax.experimental.pallas import tpu as pltpu

D_MODEL = 1024
DEPTH = 2
MLA_HEADS = 8
MLA_Q_LORA = 384
MLA_KV_LORA = 256
MLA_NOPE = 64
MLA_ROPE = 32
MLA_V = 64
MLA_QK = MLA_NOPE + MLA_ROPE
ROPE_THETA = 10000.0
RG_WIDTH = 512
RG_BLOCKS = 8
RG_C = 8.0
GDN_HEADS = 4
GDN_DK = 128
GDN_DV = 128
GDN_CHUNK = 64
D_FF = 2816
EPS = 1e-6

LANE = 128
HEAD_PAD = 128
VMEM_LIMIT = 56 * 1024 * 1024
FFN_CK = 256
N_STREAMS = 2 * GDN_HEADS
GDN_PAIRS_PER_STEP = 4

F32 = jnp.float32
BF16 = jnp.bfloat16


def _cparams(*sem):
    return pltpu.CompilerParams(dimension_semantics=sem, vmem_limit_bytes=VMEM_LIMIT)


def _const_spec(shape):
    nd = len(shape)
    return pl.BlockSpec(shape, lambda *_: (0,) * nd, pipeline_mode=pl.Buffered(1))


def _sigmoid(x):
    return 0.5 * jnp.tanh(0.5 * x) + 0.5


def _silu(x):
    h = 0.5 * x
    return h * jnp.tanh(h) + h


def _softplus(x):
    return jnp.maximum(x, 0.0) + jnp.log(1.0 + jnp.exp(-jnp.abs(x)))


def _gelu_tanh(x):
    c = math.sqrt(2.0 / math.pi)
    return 0.5 * x * (1.0 + jnp.tanh(c * (x + 0.044715 * (x * x * x))))


def _rms(x, g):
    return x * lax.rsqrt(jnp.mean(x * x, axis=-1, keepdims=True) + EPS) * g


def _dot(a, b):
    return jnp.dot(a, b, preferred_element_type=F32)


def _dot_nt(a, b):
    return lax.dot_general(a, b, (((1,), (1,)), ((), ())), preferred_element_type=F32)


def _dot_tn(a, b):
    return lax.dot_general(a, b, (((0,), (0,)), ((), ())), preferred_element_type=F32)


def _split3(x):
    x1 = x.astype(BF16)
    r1 = x - x1.astype(F32)
    x2 = r1.astype(BF16)
    x3 = (r1 - x2.astype(F32)).astype(BF16)
    return x1, x2, x3


def _mod_kernel(c_ref, w_ref, b_ref, o_ref):
    c = c_ref[...]
    a1, a2, a3 = _split3(_silu(c))
    w1, w2, w3 = _split3(w_ref[...])
    acc = _dot(a1, w1) + (_dot(a1, w2) + _dot(a2, w1)) + (_dot(a1, w3) + _dot(a2, w2) + _dot(a3, w1))
    o_ref[...] = acc + b_ref[...]


def _modulation(c, w_mod, b_mod):
    L, D, N = w_mod.shape
    Bt = c.shape[0]
    tn = 1024
    return pl.pallas_call(
        _mod_kernel,
        grid=(L, N // tn),
        in_specs=[pl.BlockSpec((Bt, D), lambda l, j: (0, 0)),
                  pl.BlockSpec((None, D, tn), lambda l, j: (l, 0, j)),
                  pl.BlockSpec((None, 1, tn), lambda l, j: (l, 0, j))],
        out_specs=pl.BlockSpec((None, Bt, tn), lambda l, j: (l, 0, j)),
        out_shape=jax.ShapeDtypeStruct((L, Bt, N), F32),
        compiler_params=_cparams("arbitrary", "arbitrary"),
        name="modulation",
    )(c, w_mod, b_mod.reshape(L, 1, N))


W_MLA, W_RG, W_GDN, W_AB, W_GATES = 768, 1024, 2048, 128, 3 * D_MODEL
_IN_OFFS = np.cumsum([0, W_MLA, W_RG, W_GDN, W_AB, W_GATES])


def _inproj_kernel(x_ref, mod_ref, g_ref, w_ref,
                   mla_ref, rg_ref, gdn_ref, ab_ref, gates_ref):
    D = D_MODEL
    x = x_ref[...]
    sh = mod_ref[:, 0:D]
    sc = mod_ref[:, D:2 * D]
    h = (_rms(x, g_ref[...]) * (1.0 + sc) + sh).astype(BF16)
    o = _IN_OFFS
    mla_ref[...] = _dot(h, w_ref[:, o[0]:o[1]]).astype(BF16)
    rg_ref[...] = _dot(h, w_ref[:, o[1]:o[2]]).astype(BF16)
    gdn_ref[...] = _dot(h, w_ref[:, o[2]:o[3]]).astype(BF16)
    ab_ref[...] = _dot(h, w_ref[:, o[3]:o[4]])
    gates_ref[...] = _dot(h, w_ref[:, o[4]:o[5]]).astype(BF16)


def _in_projection(x2d, mod, ln_g, w1, S, tm):
    T, D = x2d.shape
    per_seq = S // tm
    row = lambda i: (i, 0)
    outs = [(W_MLA, BF16), (W_RG, BF16), (W_GDN, BF16), (W_AB, F32), (W_GATES, BF16)]
    return pl.pallas_call(
        _inproj_kernel,
        grid=(T // tm,),
        in_specs=[pl.BlockSpec((tm, D), row),
                  pl.BlockSpec((None, 1, 6 * D), lambda i: (i // per_seq, 0, 0)),
                  _const_spec((1, D)),
                  _const_spec(w1.shape)],
        out_specs=[pl.BlockSpec((tm, w), row) for w, _ in outs],
        out_shape=[jax.ShapeDtypeStruct((T, w), dt) for w, dt in outs],
        compiler_params=_cparams("arbitrary"),
        name="in_projection",
    )(x2d, mod, ln_g, w1)


def _mla_kernel(in_ref, csq_ref, csk_ref, gq_ref, wq_ref, gkv_ref, wk_ref, wv_ref,
                o_ref, q_s, k_s, v_s, *, S, tq, tr):
    hp = pl.program_id(1)

    @pl.when(hp == 0)
    def _project():
        for r in range(S // tr):
            rows = slice(r * tr, (r + 1) * tr)
            qd = in_ref[rows, 0:MLA_Q_LORA].astype(F32)
            ckv = in_ref[rows, MLA_Q_LORA:MLA_Q_LORA + MLA_KV_LORA].astype(F32)
            kr = in_ref[rows, 640:768].astype(F32) * csk_ref[rows, :]
            cq = _rms(qd, gq_ref[...]).astype(BF16)
            ckvn = _rms(ckv, gkv_ref[...]).astype(BF16)
            q = _dot(cq, wq_ref[...])
            kin = jnp.concatenate([ckvn, kr.astype(BF16)], axis=1)
            k = _dot(kin, wk_ref[...])
            v = _dot(ckvn, wv_ref[...])
            cs = csq_ref[rows, :]
            for p in range(MLA_HEADS // 2):
                for hh in range(2):
                    h = 2 * p + hh
                    q_s[p, rows, hh * LANE:(hh + 1) * LANE] = (
                        q[:, h * LANE:(h + 1) * LANE] * cs).astype(BF16)
                k_s[p, rows, :] = k[:, 2 * p * LANE:(2 * p + 2) * LANE].astype(BF16)
                v_s[p, rows, :] = v[:, 2 * p * LANE:(2 * p + 2) * LANE].astype(BF16)

    for qt in range(S // tq):
        rows = slice(qt * tq, (qt + 1) * tq)
        acc = None
        for hh in range(2):
            lanes = slice(hh * LANE, (hh + 1) * LANE)
            q = q_s[hp, rows, lanes]
            k = k_s[hp, :, lanes]
            s = _dot_nt(q, k)
            m = jnp.max(s, axis=-1, keepdims=True)
            p = jnp.exp(s - m)
            l = jnp.sum(p, axis=-1, keepdims=True)
            o = _dot(p.astype(BF16), v_s[hp, :, lanes]) * (1.0 / l)
            acc = o if acc is None else acc + o
        o_ref[rows, :] = acc.astype(BF16)


def _mla(mla_in, csq, csk, gq, wq, gkv, wk, wv, B, S):
    tq = min(512, S)
    tr = min(512, S)
    npair = MLA_HEADS // 2
    kern = functools.partial(_mla_kernel, S=S, tq=tq, tr=tr)
    return pl.pallas_call(
        kern,
        grid=(B, npair),
        in_specs=[pl.BlockSpec((S, W_MLA), lambda b, p: (b, 0)),
                  _const_spec(csq.shape), _const_spec(csk.shape),
                  _const_spec(gq.shape), _const_spec(wq.shape),
                  _const_spec(gkv.shape), _const_spec(wk.shape), _const_spec(wv.shape)],
        out_specs=pl.BlockSpec((S, LANE), lambda b, p: (b, p)),
        out_shape=jax.ShapeDtypeStruct((B * S, MLA_HEADS * MLA_V), BF16),
        scratch_shapes=[pltpu.VMEM((npair, S, 2 * LANE), BF16),
                        pltpu.VMEM((npair, S, 2 * LANE), BF16),
                        pltpu.VMEM((npair, S, 2 * LANE), BF16)],
        compiler_params=_cparams("arbitrary", "arbitrary"),
        name="mla_attention",
    )(mla_in, csq, csk, gq, wq, gkv, wk, wv)


def _conv4_window(xw, w_ref, R):
    acc = xw[7:7 + R] * w_ref[0:1, :]
    acc = acc + xw[8:8 + R] * w_ref[1:2, :]
    acc = acc + xw[9:9 + R] * w_ref[2:3, :]
    acc = acc + xw[10:10 + R] * w_ref[3:4, :]
    return acc


def _scan_rows(a, b, carry, reverse):
    R, W = a.shape
    a = a.reshape(R // 8, 8, W)
    b = b.reshape(R // 8, 8, W)
    sub = lax.broadcasted_iota(jnp.int32, a.shape, 1)
    for d in (1, 2, 4):
        shift, ok = (8 - d, sub < 8 - d) if reverse else (d, sub >= d)
        a_sh = pltpu.roll(a, shift, 1)
        b_sh = pltpu.roll(b, shift, 1)
        b = b + a * jnp.where(ok, b_sh, 0.0)
        a = a * jnp.where(ok, a_sh, 1.0)
    tiles = [None] * (R // 8)
    for j in (range(R // 8 - 1, -1, -1) if reverse else range(R // 8)):
        t = b[j] + a[j] * carry
        tiles[j] = t
        carry = t[0:1] if reverse else t[7:8]
    return jnp.concatenate(tiles, axis=0), carry


def _rglru_kernel(in_ref, cw_ref, cb_ref, wg_ref, bg_ref, lam_ref, o_ref,
                  xpad, hf_s, xc_s, carry_s, *, S, R):
    W = RG_WIDTH
    nchunk = S // R
    xpad[0:8, :] = jnp.zeros((8, W), F32)
    xpad[S + 8:S + 16, :] = jnp.zeros((8, W), F32)
    xpad[8:S + 8, :] = in_ref[:, 0:W].astype(F32)
    nsp = -RG_C * _softplus(-lam_ref[...])

    def direction(d, n):
        r0 = pl.multiple_of(n * R, R)
        if d == 0:
            xc = _conv4_window(xpad[pl.ds(r0, R + 16), :], cw_ref, R) + cb_ref[...]
            xc_s[pl.ds(r0, R), :] = xc
        else:
            xc = xc_s[pl.ds(r0, R), :]
        gates = _dot(xc.astype(BF16), wg_ref[:, 2 * d * W:(2 * d + 2) * W]) \
            + bg_ref[:, 2 * d * W:(2 * d + 2) * W]
        r = _sigmoid(gates[:, :W])
        i = _sigmoid(gates[:, W:])
        log_a = nsp[d:d + 1, :] * r
        a = jnp.exp(log_a)
        th = jnp.tanh(log_a)
        m2 = -2.0 * th / (1.0 - th)
        mult = jnp.where(m2 > 0.0, m2 * lax.rsqrt(m2), 0.0)
        row8 = lax.broadcasted_iota(jnp.int32, (8, W), 0)
        if d == 0:
            edge = jnp.where((row8 == 0) & (n == 0), 1.0, mult[0:8])
            mult = jnp.concatenate([edge, mult[8:]], axis=0)
        else:
            edge = jnp.where((row8 == 7) & (n == nchunk - 1), 1.0, mult[R - 8:])
            mult = jnp.concatenate([mult[:R - 8], edge], axis=0)
        h, carry = _scan_rows(a, mult * (i * xc), carry_s[...], reverse=(d == 1))
        carry_s[...] = carry
        return r0, h

    carry_s[...] = jnp.zeros((1, W), F32)

    def fwd(n, c):
        r0, h = direction(0, n)
        hf_s[pl.ds(r0, R), :] = h
        return c

    lax.fori_loop(0, nchunk, fwd, 0)
    carry_s[...] = jnp.zeros((1, W), F32)

    def bwd(n, c):
        r0, h = direction(1, nchunk - 1 - n)
        gate = in_ref[pl.ds(r0, R), W:2 * W].astype(F32)
        o_ref[pl.ds(r0, R), :] = ((hf_s[pl.ds(r0, R), :] + h) * _gelu_tanh(gate)).astype(BF16)
        return c

    lax.fori_loop(0, nchunk, bwd, 0)


def _rglru(rg_in, cw, cb, wg, bg, lam, B, S):
    R = min(256, S)
    kern = functools.partial(_rglru_kernel, S=S, R=R)
    return pl.pallas_call(
        kern,
        grid=(B,),
        in_specs=[pl.BlockSpec((S, W_RG), lambda b: (b, 0)),
                  _const_spec(cw.shape), _const_spec(cb.shape), _const_spec(wg.shape),
                  _const_spec(bg.shape), _const_spec(lam.shape)],
        out_specs=pl.BlockSpec((S, RG_WIDTH), lambda b: (b, 0)),
        out_shape=jax.ShapeDtypeStruct((B * S, RG_WIDTH), BF16),
        scratch_shapes=[pltpu.VMEM((S + 16, RG_WIDTH), F32),
                        pltpu.VMEM((S, RG_WIDTH), F32),
                        pltpu.VMEM((S, RG_WIDTH), F32),
                        pltpu.VMEM((1, RG_WIDTH), F32)],
        compiler_params=_cparams("arbitrary"),
        name="rglru",
    )(rg_in, cw, cb, wg, bg, lam)


def _tri_consts():
    C = GDN_CHUNK
    r = lax.broadcasted_iota(jnp.int32, (2 * C, 2 * C), 0)
    c = lax.broadcasted_iota(jnp.int32, (2 * C, 2 * C), 1)
    same = (r // C) == (c // C)
    one = lambda m: jnp.where(m, 1.0, 0.0).astype(BF16)
    return r, c, same, one


def _unit_tri_inverse(a_list, ri, ci, eye, width):
    blk = lambda n: (ri // n) == (ci // n)
    ident = jnp.where(eye, 1.0, 0.0)
    ps = [jnp.where(blk(8), -a, 0.0).astype(BF16) for a in a_list]
    ts = [ident + p.astype(F32) for p in ps]
    for _ in range(2):
        ps = [_dot(p, p).astype(BF16) for p in ps]
        ts = [t + _dot(t.astype(BF16), p) for t, p in zip(ts, ps)]
    n = 8
    while n < width:
        sel = blk(2 * n) & jnp.logical_not(blk(n))
        offs = [jnp.where(sel, a, 0.0).astype(BF16) for a in a_list]
        tbs = [t.astype(BF16) for t in ts]
        mids = [_dot(tb, off).astype(BF16) for tb, off in zip(tbs, offs)]
        ts = [t - _dot(mid, tb) for t, mid, tb in zip(ts, mids, tbs)]
        n *= 2
    return ts


def _gdn_kernel(in_ref, ab_ref, cw_ref, alr_ref, dtr_ref, ng_ref,
                o_ref, xpad, qn_s, kn_s, vn_s, gcol_s, rcol_s, bcol_s, grow_s, rrow_s, brow_s,
                m1_s, m2_s, st_s, o_s, *, S):
    C = GDN_CHUNK
    H = GDN_HEADS
    NC = S // C
    HW = H * GDN_DK
    R = min(256, S)

    xpad[0:8, :] = jnp.zeros((8, HW), F32)
    xpad[S + 8:S + 16, :] = jnp.zeros((8, HW), F32)
    for grp, dst in enumerate((qn_s, kn_s, vn_s)):
        xpad[8:S + 8, :] = in_ref[:, grp * HW:(grp + 1) * HW].astype(F32)

        def conv_body(n, c, grp=grp, dst=dst):
            r0 = pl.multiple_of(n * R, R)
            xw = xpad[pl.ds(r0, R + 16), :]
            y = _silu(_conv4_window(xw, cw_ref.at[:, grp * HW:(grp + 1) * HW], R))
            if grp < 2:
                for h in range(H):
                    t = y[:, h * LANE:(h + 1) * LANE]
                    t = t * lax.rsqrt(jnp.sum(t * t, axis=-1, keepdims=True) + EPS)
                    dst[pl.ds(r0, R), h * LANE:(h + 1) * LANE] = t.astype(BF16)
            else:
                dst[pl.ds(r0, R), :] = y.astype(BF16)
            return c

        lax.fori_loop(0, S // R, conv_body, 0)

    r, c, same, one = _tri_consts()
    lane = lax.broadcasted_iota(jnp.int32, (2 * C, LANE), 1)
    is_fwd_lane = lane < H
    lane_lo = lax.broadcasted_iota(jnp.int32, (N_STREAMS, LANE), 1) < C
    tl_incl = one(same & (c <= r))
    tu_incl = one(same & (c >= r))

    def sum3(parts, mat):
        acc = None
        for p_ in parts:
            t = _dot(mat, p_)
            acc = t if acc is None else acc + t
        return acc

    for pr in range(S // (2 * C)):
        rows = slice(pr * 2 * C, (pr + 1) * 2 * C)
        ab = ab_ref[rows, :]
        g_col = -jnp.exp(alr_ref[...]) * _softplus(ab + dtr_ref[...])
        parts = _split3(g_col)
        s_lo, s_hi = sum3(parts, tl_incl), sum3(parts, tu_incl)
        gcol_s[rows, :] = jnp.where(is_fwd_lane, s_lo, s_hi)
        rcol_s[rows, :] = jnp.where(is_fwd_lane, s_hi, s_lo) - g_col
        bcol_s[rows, :] = _sigmoid(ab)
        for src, dst, lo in ((gcol_s, grow_s, 0), (rcol_s, rrow_s, 0), (bcol_s, brow_s, N_STREAMS)):
            t = src[rows, :].T[lo:lo + N_STREAMS, :]
            sw = pltpu.roll(t, C, 1)
            dst[2 * pr] = jnp.where(lane_lo, t, sw)
            dst[2 * pr + 1] = jnp.where(lane_lo, sw, t)

    P = 2 * C
    ri = lax.broadcasted_iota(jnp.int32, (P, P), 0)
    ci = lax.broadcasted_iota(jnp.int32, (P, P), 1)
    eye = ri == ci
    top = ri < C
    same_dir = (ri // C) == (ci // C)
    bot = jnp.logical_not(top)
    incl = same_dir & ((top & (ci <= ri)) | (bot & (ci >= ri)))
    strict = same_dir & ((top & (ci < ri)) | (bot & (ci > ri)))
    lane_f = lax.broadcasted_iota(jnp.int32, (H, LANE), 1) < C
    scale = GDN_DK ** -0.5

    def pair_rows(ref, rf, rb, lanes):
        return jnp.concatenate([ref[pl.ds(rf, C), lanes], ref[pl.ds(rb, C), lanes]], axis=0)

    def pair_lanes(arr_s, n, nb):
        return jnp.where(lane_f, arr_s[n][0:H], arr_s[nb][H:2 * H])

    def col_pair(col_f, col_b, lane_idx_f, lane_idx_b):
        return jnp.concatenate([jnp.broadcast_to(col_f[:, lane_idx_f:lane_idx_f + 1], (C, P)),
                                jnp.broadcast_to(col_b[:, lane_idx_b:lane_idx_b + 1], (C, P))], axis=0)

    G = GDN_PAIRS_PER_STEP

    def phase1(i):
        a_l, decay_l, qk_l, brow_l = [], [], [], []
        for n in [G * i + g for g in range(G)]:
            nb = NC - 1 - n
            rf = pl.multiple_of(n * C, C)
            rb = pl.multiple_of(nb * C, C)
            gcol_f, gcol_b = gcol_s[pl.ds(rf, C), :], gcol_s[pl.ds(rb, C), :]
            bcol_f, bcol_b = bcol_s[pl.ds(rf, C), :], bcol_s[pl.ds(rb, C), :]
            grow = pair_lanes(grow_s, n, nb)
            brow = pair_lanes(brow_s, n, nb)
            for h in range(H):
                lanes = slice(h * LANE, (h + 1) * LANE)
                kp = pair_rows(kn_s, rf, rb, lanes)
                qp = pair_rows(qn_s, rf, rb, lanes)
                kq = _dot_nt(jnp.concatenate([kp, qp], axis=0), kp)
                gi = col_pair(gcol_f, gcol_b, h, H + h)
                bi = col_pair(bcol_f, bcol_b, N_STREAMS + h, N_STREAMS + H + h)
                decay = jnp.where(incl, jnp.exp(jnp.minimum(gi - grow[h:h + 1, :], 0.0)), 0.0)
                a_l.append(jnp.where(strict, bi * kq[0:P] * decay, 0.0))
                decay_l.append(decay)
                qk_l.append(kq[P:2 * P])
                brow_l.append(brow[h:h + 1, :])
        t_l = _unit_tri_inverse(a_l, ri, ci, eye, C)
        for j in range(G * H):
            m1_s[G * i * H + j] = (t_l[j] * brow_l[j]).astype(BF16)
            m2_s[G * i * H + j] = (qk_l[j] * decay_l[j] * scale).astype(BF16)

    st_s[...] = jnp.zeros(st_s.shape, F32)
    o_s[...] = jnp.zeros(o_s.shape, F32)

    def phase2(n):
        nb = NC - 1 - n
        rf = pl.multiple_of(n * C, C)
        rb = pl.multiple_of(nb * C, C)
        eg_f, eg_b = jnp.exp(gcol_s[pl.ds(rf, C), :]), jnp.exp(gcol_s[pl.ds(rb, C), :])
        ek_f, ek_b = jnp.exp(rcol_s[pl.ds(rf, C), :]), jnp.exp(rcol_s[pl.ds(rb, C), :])
        etot_f = jnp.exp(grow_s[n] + rrow_s[n])
        etot_b = jnp.exp(grow_s[nb] + rrow_s[nb])
        kp_l, kqs_l, vp_l = [], [], []
        for h in range(H):
            lanes = slice(h * LANE, (h + 1) * LANE)
            kp = pair_rows(kn_s, rf, rb, lanes)
            qp = pair_rows(qn_s, rf, rb, lanes)
            kp_l.append(kp)
            vp_l.append(pair_rows(vn_s, rf, rb, lanes))
            kqs_l.append(_dot(jnp.concatenate([kp, qp], axis=0), st_s[h].astype(BF16)))
        vnew_l, oq_l = [], []
        for h in range(H):
            kqs = kqs_l[h]
            ks = jnp.concatenate([kqs[0:C, 0:LANE], kqs[C:P, LANE:2 * LANE]], axis=0)
            qs = jnp.concatenate([kqs[P:P + C, 0:LANE], kqs[P + C:2 * P, LANE:2 * LANE]], axis=0)
            egc = col_pair(eg_f, eg_b, h, H + h)
            oq_l.append(egc * scale * qs)
            r1 = (vp_l[h].astype(F32) - egc * ks).astype(BF16)
            vnew_l.append(_dot(m1_s[n * H + h], r1))
        for h in range(H):
            lanes = slice(h * LANE, (h + 1) * LANE)
            v_new = vnew_l[h]
            o = _dot(m2_s[n * H + h], v_new.astype(BF16)) + oq_l[h]
            x = col_pair(ek_f, ek_b, h, H + h) * v_new
            xbd = jnp.concatenate([jnp.where(top, x, 0.0), jnp.where(top, 0.0, x)], axis=1).astype(BF16)
            etot = jnp.concatenate([etot_f[h:h + 1, :], etot_b[H + h:H + h + 1, :]], axis=1)
            st_s[h] = st_s[h] * etot + _dot_tn(kp_l[h], xbd)
            o_s[pl.ds(rf, C), lanes] += o[0:C]
            o_s[pl.ds(rb, C), lanes] += o[C:P]

    nstep = NC // G
    phase1(0)

    def fused(i, carry):
        for g in range(G):
            phase2(G * i + g)
        phase1(jnp.minimum(i + 1, nstep - 1))
        return carry

    lax.fori_loop(0, nstep, fused, 0)

    def phase3(n, carry):
        r0 = pl.multiple_of(n * R, R)
        z = in_ref[pl.ds(r0, R), 3 * HW:4 * HW].astype(F32)
        for h in range(H):
            lanes = slice(h * LANE, (h + 1) * LANE)
            o = o_s[pl.ds(r0, R), lanes]
            o = _rms(o, ng_ref[...]) * _silu(z[:, lanes])
            o_ref[pl.ds(r0, R), lanes] = o.astype(BF16)
        return carry

    lax.fori_loop(0, S // R, phase3, 0)


def _gdn(gdn_in, ab, cw, alr, dtr, ng, B, S):
    C = GDN_CHUNK
    NC = S // C
    HW = GDN_HEADS * GDN_DK
    kern = functools.partial(_gdn_kernel, S=S)
    scratch = [pltpu.VMEM((S + 16, HW), F32),
               pltpu.VMEM((S, HW), BF16), pltpu.VMEM((S, HW), BF16), pltpu.VMEM((S, HW), BF16),
               pltpu.VMEM((S, LANE), F32), pltpu.VMEM((S, LANE), F32), pltpu.VMEM((S, LANE), F32),
               pltpu.VMEM((NC, N_STREAMS, LANE), F32), pltpu.VMEM((NC, N_STREAMS, LANE), F32),
               pltpu.VMEM((NC, N_STREAMS, LANE), F32),
               pltpu.VMEM((NC * GDN_HEADS, 2 * C, 2 * C), BF16),
               pltpu.VMEM((NC * GDN_HEADS, 2 * C, 2 * C), BF16),
               pltpu.VMEM((GDN_HEADS, GDN_DK, 2 * GDN_DV), F32),
               pltpu.VMEM((S, HW), F32)]
    return pl.pallas_call(
        kern,
        grid=(B,),
        in_specs=[pl.BlockSpec((S, W_GDN), lambda b: (b, 0)),
                  pl.BlockSpec((S, W_AB), lambda b: (b, 0)),
                  _const_spec(cw.shape), _const_spec(alr.shape), _const_spec(dtr.shape),
                  _const_spec(ng.shape)],
        out_specs=pl.BlockSpec((S, HW), lambda b: (b, 0)),
        out_shape=jax.ShapeDtypeStruct((B * S, HW), BF16),
        scratch_shapes=scratch,
        compiler_params=_cparams("arbitrary"),
        name="gated_deltanet",
    )(gdn_in, ab, cw, alr, dtr, ng)


def _merge_kernel(x_ref, mod_ref, om_ref, or_ref, og_ref, gates_ref, wb_ref, wo_ref, o_ref):
    D = D_MODEL
    mixed = None
    for n, src in enumerate((om_ref, or_ref, og_ref)):
        g = _sigmoid(gates_ref[:, n * D:(n + 1) * D].astype(F32))
        term = g * _dot(src[...], wb_ref[n])
        mixed = term if mixed is None else mixed + term
    gt1 = mod_ref[:, 2 * D:3 * D]
    o_ref[...] = x_ref[...] + gt1 * _dot(mixed.astype(BF16), wo_ref[...])


def _merge(x2d, mod, o_mla, o_rg, o_gdn, gates, wb, wo, S, tm):
    T, D = x2d.shape
    per_seq = S // tm
    row = lambda i: (i, 0)
    return pl.pallas_call(
        _merge_kernel,
        grid=(T // tm,),
        in_specs=[pl.BlockSpec((tm, D), row),
                  pl.BlockSpec((None, 1, 6 * D), lambda i: (i // per_seq, 0, 0)),
                  pl.BlockSpec((tm, 512), row), pl.BlockSpec((tm, 512), row), pl.BlockSpec((tm, 512), row),
                  pl.BlockSpec((tm, W_GATES), row),
                  _const_spec(wb.shape), _const_spec(wo.shape)],
        out_specs=pl.BlockSpec((tm, D), row),
        out_shape=jax.ShapeDtypeStruct((T, D), F32),
        compiler_params=_cparams("arbitrary"),
        name="branch_merge",
    )(x2d, mod, o_mla, o_rg, o_gdn, gates, wb, wo)


def _ffn_kernel(x_ref, xp_ref, xn_ref, mod_ref, g_ref, wu_ref, cw_ref, cb_ref, wd_ref, fg_ref,
                o_ref, act_s, *, tm, per_seq, final):
    D = D_MODEL
    i = pl.program_id(0)
    sh = mod_ref[:, 3 * D:4 * D]
    sc = mod_ref[:, 4 * D:5 * D]
    gt2 = mod_ref[:, 5 * D:6 * D]
    x = x_ref[...]
    xall = jnp.concatenate([x, xp_ref[...], xn_ref[...]], axis=0)
    hcat = (_rms(xall, g_ref[...]) * (1.0 + sc) + sh).astype(BF16)
    has_prev = jnp.where(i % per_seq == 0, 0.0, 1.0)
    has_next = jnp.where(i % per_seq == per_seq - 1, 0.0, 1.0)
    row8 = lax.broadcasted_iota(jnp.int32, (8, 2 * FFN_CK), 0)
    nck = D_FF // FFN_CK
    split = (nck + 1) // 2
    acc = None
    for j in range(nck):
        up_all = _dot(hcat, wu_ref[j])
        up = up_all[:tm]
        prev = up_all[tm + 7:tm + 8, :] * has_prev
        nxt = up_all[tm + 8:tm + 9, :] * has_next
        dn = pltpu.roll(up, 1, 0)
        dn = jnp.concatenate([jnp.where(row8 == 0, prev, dn[0:8]), dn[8:]], axis=0)
        un = pltpu.roll(up, tm - 1, 0)
        un = jnp.concatenate([un[:tm - 8], jnp.where(row8 == 7, nxt, un[tm - 8:])], axis=0)
        cw = cw_ref[j]
        y = dn * cw[0:1, :] + up * cw[1:2, :] + un * cw[2:3, :] + cb_ref[j]
        act = _silu(y[:, FFN_CK:]) * y[:, :FFN_CK]
        act_s[:, j * FFN_CK:(j + 1) * FFN_CK] = act.astype(BF16)
        if j == split - 1:
            acc = _dot(act_s[:, 0:split * FFN_CK], wd_ref[0:split * FFN_CK, :])
    acc = acc + _dot(act_s[:, split * FFN_CK:], wd_ref[split * FFN_CK:, :])
    y = x + gt2 * acc
    if final:
        y = _rms(y, fg_ref[...])
    o_ref[...] = y


def _ffn(x2d, mod, ln_g, wu, cw, cb, wd, fg, S, tm, final):
    T, D = x2d.shape
    per_seq = S // tm
    hb = tm // 8
    nblk8 = T // 8
    kern = functools.partial(_ffn_kernel, tm=tm, per_seq=per_seq, final=final)
    return pl.pallas_call(
        kern,
        grid=(T // tm,),
        in_specs=[pl.BlockSpec((tm, D), lambda i: (i, 0)),
                  pl.BlockSpec((8, D), lambda i: (jnp.maximum(i * hb - 1, 0), 0)),
                  pl.BlockSpec((8, D), lambda i: (jnp.minimum((i + 1) * hb, nblk8 - 1), 0)),
                  pl.BlockSpec((None, 1, 6 * D), lambda i: (i // per_seq, 0, 0)),
                  _const_spec((1, D)),
                  _const_spec(wu.shape), _const_spec(cw.shape), _const_spec(cb.shape),
                  _const_spec(wd.shape), _const_spec((1, D))],
        out_specs=pl.BlockSpec((tm, D), lambda i: (i, 0)),
        out_shape=jax.ShapeDtypeStruct((T, D), F32),
        scratch_shapes=[pltpu.VMEM((tm, D_FF), BF16)],
        compiler_params=_cparams("arbitrary"),
        name="conv_glu_ffn",
    )(x2d, x2d, x2d, mod, ln_g, wu, cw, cb, wd, fg)


def _rot_cols(w):
    half = w.shape[-1] // 2
    return jnp.concatenate([-w[..., half:], w[..., :half]], axis=-1)


def _prep_layer(l, p):
    D = D_MODEL
    w = p['w_in'][l]
    o = np.cumsum([0, MLA_Q_LORA, MLA_KV_LORA + MLA_ROPE, RG_WIDTH, RG_WIDTH, 3 * 512, 512, 8, 8, 3 * D])
    kro = o[1] + MLA_KV_LORA
    w_kr = w[:, kro:o[2]]
    w1 = jnp.concatenate([
        w[:, o[0]:o[1]], w[:, o[1]:kro], w_kr, _rot_cols(w_kr), jnp.zeros((D, 64), F32),
        w[:, o[2]:o[4]],
        w[:, o[4]:o[6]],
        w[:, o[6]:o[8]], jnp.zeros((D, W_AB - 16), F32),
        w[:, o[8]:o[9]]], axis=1).astype(BF16)

    H = MLA_HEADS
    wq = p['mla_w_uq'][l].reshape(MLA_Q_LORA, H, MLA_QK)
    rope = wq[..., MLA_NOPE:]
    wq_ext = jnp.concatenate([wq[..., :MLA_NOPE], rope, _rot_cols(rope)], axis=-1)
    wq_ext = wq_ext.reshape(MLA_Q_LORA, H * HEAD_PAD).astype(BF16)
    wkv = p['mla_w_ukv'][l].reshape(MLA_KV_LORA, H, MLA_NOPE + MLA_V)
    wk_top = jnp.concatenate([wkv[..., :MLA_NOPE], jnp.zeros((MLA_KV_LORA, H, 64), F32)], axis=-1)
    e = np.zeros((128, H, HEAD_PAD), np.float32)
    for j in range(MLA_ROPE):
        for src in (j, MLA_ROPE + j):
            e[src, :, MLA_NOPE + j] = 1.0
            e[src, :, MLA_NOPE + MLA_ROPE + j] = 1.0
    wk_ext = jnp.concatenate([wk_top, jnp.asarray(e)], axis=0).reshape(MLA_KV_LORA + 128, H * HEAD_PAD).astype(BF16)
    wv = wkv[..., MLA_NOPE:]
    z = jnp.zeros_like(wv)
    even = (np.arange(H) % 2 == 0)[None, :, None]
    wv_ext = jnp.concatenate([jnp.where(even, wv, z), jnp.where(even, z, wv)], axis=-1)
    wv_ext = wv_ext.reshape(MLA_KV_LORA, H * HEAD_PAD).astype(BF16)

    eye = jnp.eye(RG_BLOCKS, dtype=F32)
    bd = lambda t: jnp.einsum('ncd,nm->ncmd', t, eye).reshape(RG_WIDTH, RG_WIDTH)
    wg = jnp.concatenate([bd(p['rg_w_a'][l, 0]), bd(p['rg_w_i'][l, 0]),
                          bd(p['rg_w_a'][l, 1]), bd(p['rg_w_i'][l, 1])], axis=1).astype(BF16)
    bg = jnp.concatenate([p['rg_b_a'][l, 0], p['rg_b_i'][l, 0], p['rg_b_a'][l, 1], p['rg_b_i'][l, 1]])[None, :]

    pad_row = lambda v: jnp.concatenate([v.reshape(-1), jnp.zeros((LANE - N_STREAMS,), F32)])[None, :]

    nck = D_FF // FFN_CK
    pair = lambda t: jnp.concatenate([t[..., :D_FF].reshape(t.shape[:-1] + (nck, FFN_CK)),
                                      t[..., D_FF:].reshape(t.shape[:-1] + (nck, FFN_CK))], axis=-1)
    wu = jnp.moveaxis(pair(p['ffn_w_up'][l]), 1, 0).astype(BF16)
    fcw = jnp.moveaxis(pair(p['ffn_conv_w'][l]), 1, 0)
    fcb = pair(p['ffn_conv_b'][l])[:, None, :]
    wd = p['ffn_w_down'][l].astype(BF16)

    return dict(
        ln1_g=p['ln1_g'][l][None, :], w1=w1,
        gq=p['mla_q_norm_g'][l][None, :], wq=wq_ext, gkv=p['mla_kv_norm_g'][l][None, :],
        wk=wk_ext, wv=wv_ext,
        rg_cw=p['rg_conv_w'][l], rg_cb=p['rg_conv_b'][l][None, :], wg=wg, bg=bg, lam=p['rg_lam'][l],
        gdn_cw=p['gdn_conv_w'][l], alr=pad_row(p['gdn_a_log'][l]), dtr=pad_row(p['gdn_dt_bias'][l]),
        ng=p['gdn_norm_g'][l][None, :],
        wb=p['w_branch'][l].astype(BF16), wo=p['w_out'][l].astype(BF16),
        ln2_g=p['ln2_g'][l][None, :], wu=wu, fcw=fcw, fcb=fcb, wd=wd)


def _rope_tables(S):
    inv = 1.0 / (ROPE_THETA ** (jnp.arange(0, MLA_ROPE, 2, dtype=F32) / MLA_ROPE))
    ang = jnp.arange(S, dtype=F32)[:, None] * inv[None, :]
    cos, sin = jnp.cos(ang), jnp.sin(ang)
    scale = MLA_QK ** -0.5
    csq = scale * jnp.concatenate([jnp.ones((S, MLA_NOPE), F32), cos, cos, sin, sin], axis=1)
    csk = jnp.concatenate([cos, cos, sin, sin, jnp.zeros((S, 64), F32)], axis=1)
    return csq, csk


def _trunk(x, mod, layers, final_g, csq, csk):
    B, S, D = x.shape
    x2d = x.reshape(B * S, D)
    tm = min(512, S)
    for l, w in enumerate(layers):
        m = mod[l].reshape(B, 1, 6 * D)
        mla_in, rg_in, gdn_in, ab, gates = _in_projection(x2d, m, w['ln1_g'], w['w1'], S, tm)
        o_mla = _mla(mla_in, csq, csk, w['gq'], w['wq'], w['gkv'], w['wk'], w['wv'], B, S)
        o_rg = _rglru(rg_in, w['rg_cw'], w['rg_cb'], w['wg'], w['bg'], w['lam'], B, S)
        o_gdn = _gdn(gdn_in, ab, w['gdn_cw'], w['alr'], w['dtr'], w['ng'], B, S)
        x2d = _merge(x2d, m, o_mla, o_rg, o_gdn, gates, w['wb'], w['wo'], S, tm)
        x2d = _ffn(x2d, m, w['ln2_g'], w['wu'], w['fcw'], w['fcb'], w['wd'], final_g, S, min(1024, S),
                   final=(l == len(layers) - 1))
    return x2d.reshape(B, S, D)


def kernel(x_prompt, x_sample, c_prompt, c_sample, ln1_g, w_mod, b_mod, w_in, mla_q_norm_g, mla_w_uq, mla_kv_norm_g, mla_w_ukv, rg_conv_w, rg_conv_b, rg_w_a, rg_b_a, rg_w_i, rg_b_i, rg_lam, gdn_conv_w, gdn_a_log, gdn_dt_bias, gdn_norm_g, w_branch, w_out, ln2_g, ffn_w_up, ffn_conv_w, ffn_conv_b, ffn_w_down, final_norm_g):
    p = dict(ln1_g=ln1_g, w_in=w_in, mla_q_norm_g=mla_q_norm_g, mla_w_uq=mla_w_uq,
             mla_kv_norm_g=mla_kv_norm_g, mla_w_ukv=mla_w_ukv, rg_conv_w=rg_conv_w, rg_conv_b=rg_conv_b,
             rg_w_a=rg_w_a, rg_b_a=rg_b_a, rg_w_i=rg_w_i, rg_b_i=rg_b_i, rg_lam=rg_lam,
             gdn_conv_w=gdn_conv_w, gdn_a_log=gdn_a_log, gdn_dt_bias=gdn_dt_bias, gdn_norm_g=gdn_norm_g,
             w_branch=w_branch, w_out=w_out, ln2_g=ln2_g, ffn_w_up=ffn_w_up, ffn_conv_w=ffn_conv_w,
             ffn_conv_b=ffn_conv_b, ffn_w_down=ffn_w_down)
    L = w_in.shape[0]
    layers = [_prep_layer(l, p) for l in range(L)]
    Bp = x_prompt.shape[0]
    mod = _modulation(jnp.concatenate([c_prompt, c_sample], axis=0), w_mod, b_mod)
    fg = final_norm_g[None, :]
    outs = []
    for x, m in ((x_prompt, mod[:, :Bp]), (x_sample, mod[:, Bp:])):
        csq, csk = _rope_tables(x.shape[1])
        outs.append(_trunk(x, m, layers, fg, csq, csk))
    return tuple(outs)
```
